```python
import math, functools
import jax, jax.numpy as jnp
from jax import lax
import numpy as np

D_MODEL = 1024
BATCH = 8
SEQ = 8192
DEPTH = 1
DEC_BATCH = 128
DEC_SEQ = 1
PAST_LEN = 8192
PAGE_SIZE = 128

MIX_WIDTH = D_MODEL
MOBA_HEADS = 8
MOBA_HEAD_DIM = MIX_WIDTH // 2 // MOBA_HEADS
MOBA_WIDTH = MOBA_HEADS * MOBA_HEAD_DIM
MOBA_BLOCK = 256
MOBA_TOPK = 3
MOBA_QCHUNK = 64
GDN_WIDTH = MIX_WIDTH - MOBA_WIDTH
GDN_HEAD_DIM = 128
GDN_HEADS = GDN_WIDTH // GDN_HEAD_DIM
GDN_CONV = 4
GDN_CHUNK = 64
N_MEM = 256
X_HEADS = 4
X_HEAD_DIM = D_MODEL // X_HEADS
D_FF = -(-(8 * D_MODEL) // (3 * 256)) * 256
IN_WIDTH = 3 * MOBA_WIDTH + 3 * GDN_WIDTH + GDN_WIDTH + 2 * GDN_HEADS
RMS_EPS = 1e-6

kernel_name = 'moba_gdn_hymba_decoder_step'


def rmsnorm(x, w):
    xf = x.astype(jnp.float32)
    y = xf * lax.rsqrt(jnp.mean(xf * xf, axis=-1, keepdims=True) + RMS_EPS)
    return (y * w.astype(jnp.float32)).astype(x.dtype)


def l2norm(x):
    return x * lax.rsqrt(jnp.sum(x * x, axis=-1, keepdims=True) + 1e-6)


def alibi_slopes():
    return jnp.exp2(-8.0 * jnp.arange(1, MOBA_HEADS + 1, dtype=jnp.float32) / MOBA_HEADS)


def to_blocks(k):
    B, L, H, d = k.shape
    nb = -(-L // MOBA_BLOCK)
    k = jnp.pad(k, ((0, 0), (0, nb * MOBA_BLOCK - L), (0, 0), (0, 0)))
    return k.reshape(B, nb, MOBA_BLOCK, H, d).transpose(0, 3, 1, 2, 4)


def moba_core(q, t, kb, vb, kmean):
    B, H, Q, d = q.shape
    nb = kb.shape[2]
    ksel = min(MOBA_TOPK, nb)
    bt = t // MOBA_BLOCK
    gate = jnp.einsum('bhqd,bhnd->bhqn', q.astype(jnp.float32), kmean)
    fully_past = jnp.arange(nb)[None, :] < bt[:, None]
    gate = jnp.where(fully_past, gate, -jnp.inf)
    _, sel = lax.top_k(gate, ksel)
    own = jnp.broadcast_to(bt[None, None, :, None], (B, H, Q, 1)).astype(sel.dtype)
    idx = jnp.concatenate([sel, own], axis=-1)
    gather = jax.vmap(jax.vmap(lambda blocks, ix: blocks[ix]))
    kg = gather(kb, idx)
    vg = gather(vb, idx)
    pos = idx[..., None] * MOBA_BLOCK + jnp.arange(MOBA_BLOCK, dtype=idx.dtype)
    rank = jnp.arange(ksel + 1)
    slot_ok = (rank[None, :] < bt[:, None]) | (rank[None, :] == ksel)
    tq = t[None, None, :, None, None]
    mask = slot_ok[None, None, :, :, None] & (pos <= tq)
    s = jnp.einsum('bhqd,bhqnkd->bhqnk', q, kg).astype(jnp.float32) * (MOBA_HEAD_DIM ** -0.5)
    s = s - alibi_slopes()[None, :, None, None, None] * (tq - pos).astype(jnp.float32)
    s = jnp.where(mask, s, -jnp.inf)
    p = jax.nn.softmax(s.reshape(B, H, Q, -1), axis=-1).reshape(s.shape)
    return jnp.einsum('bhqnk,bhqnkd->bhqd', p.astype(vg.dtype), vg)


def moba_prompt(q, k, v):
    B, T, H, d = q.shape
    kb, vb = to_blocks(k), to_blocks(v)
    kmean = jnp.mean(kb.astype(jnp.float32), axis=3)
    nq = T // MOBA_QCHUNK
    qc = q.reshape(B, nq, MOBA_QCHUNK, H, d).transpose(1, 0, 3, 2, 4)
    tc = jnp.arange(T, dtype=jnp.int32).reshape(nq, MOBA_QCHUNK)
    out = lax.map(lambda a: moba_core(a[0], a[1], kb, vb, kmean), (qc, tc))
    return out.transpose(1, 0, 3, 2, 4).reshape(B, T, H * d)


def moba_sample(q, k_new, v_new, pool_k, pool_v, page_table):
    DB, S, H, d = q.shape
    past = page_table.shape[1] * pool_k.shape[1]
    kp = pool_k[page_table].reshape(DB, past, H, d)
    vp = pool_v[page_table].reshape(DB, past, H, d)
    kb = to_blocks(jnp.concatenate([kp, k_new.astype(kp.dtype)], axis=1))
    vb = to_blocks(jnp.concatenate([vp, v_new.astype(vp.dtype)], axis=1))
    kmean = jnp.mean(kb.astype(jnp.float32), axis=3)
    t = past + jnp.arange(S, dtype=jnp.int32)
    out = moba_core(q.transpose(0, 2, 1, 3), t, kb, vb, kmean)
    return out.transpose(0, 2, 1, 3).reshape(DB, S, H * d)


def split_chunks(x, n, c):
    B, T, H = x.shape[:3]
    rest = x.shape[3:]
    x = x.reshape((B, n, c, H) + rest)
    return x.transpose((1, 0, 3, 2) + tuple(range(4, x.ndim)))


def gdn_chunked(q, k, v, g, beta, s0, chunk):
    B, T, H, dk = q.shape
    dv = v.shape[-1]
    n = T // chunk
    qc, kc, vc = split_chunks(q, n, chunk), split_chunks(k, n, chunk), split_chunks(v, n, chunk)
    gc = jnp.cumsum(split_chunks(g, n, chunk), axis=-1)
    bc = split_chunks(beta, n, chunk)
    ix = jnp.arange(chunk)
    incl = ix[:, None] >= ix[None, :]
    strict = ix[:, None] > ix[None, :]
    decay = jnp.exp(jnp.where(incl, gc[..., :, None] - gc[..., None, :], -jnp.inf))
    kk = jnp.einsum('nbhid,nbhjd->nbhij', kc, kc)
    low = jnp.where(strict, bc[..., :, None] * kk * decay, 0.0)
    rhs = jnp.concatenate([vc * bc[..., None], kc * (bc * jnp.exp(gc))[..., None]], axis=-1)
    sol = lax.linalg.triangular_solve(low + jnp.eye(chunk, dtype=jnp.float32), rhs,
                                      left_side=True, lower=True, unit_diagonal=True)
    value, kcd = sol[..., :dv], sol[..., dv:]
    aqk = jnp.einsum('nbhid,nbhjd->nbhij', qc, kc) * decay
    q_dec = qc * jnp.exp(gc)[..., None]
    k_dec = kc * jnp.exp(gc[..., -1:] - gc)[..., None]
    g_last = jnp.exp(gc[..., -1])

    def step(s, xs):
        val, kd, a, qd, kdec, gl = xs
        v_new = val - jnp.einsum('bhcd,bhde->bhce', kd, s)
        o = jnp.einsum('bhcd,bhde->bhce', qd, s) + jnp.einsum('bhij,bhje->bhie', a, v_new)
        s = s * gl[..., None, None] + jnp.einsum('bhcd,bhce->bhde', kdec, v_new)
        return s, o

    s, o = lax.scan(step, s0, (value, kcd, aqk, q_dec, k_dec, g_last))
    return o.transpose(1, 0, 3, 2, 4).reshape(B, T, H, dv), s


def gdn_branch(qkv, z, a, b, conv_buf, s0, conv_w, a_log, dt_bias, norm_w):
    B, T, _ = qkv.shape
    xpad = jnp.concatenate([conv_buf.astype(qkv.dtype), qkv], axis=1)
    conv = sum(xpad[:, j:j + T] * conv_w[j] for j in range(GDN_CONV))
    new_buf = xpad[:, T:]
    act = jax.nn.silu(conv.astype(jnp.float32))
    q, k, v = jnp.split(act, 3, axis=-1)
    shp = (B, T, GDN_HEADS, GDN_HEAD_DIM)
    q = l2norm(q.reshape(shp)) * (GDN_HEAD_DIM ** -0.5)
    k = l2norm(k.reshape(shp))
    v = v.reshape(shp)
    beta = jax.nn.sigmoid(b.astype(jnp.float32))
    g = -jnp.exp(a_log.astype(jnp.float32)) * jax.nn.softplus(a.astype(jnp.float32) + dt_bias.astype(jnp.float32))
    chunk = GDN_CHUNK if T % GDN_CHUNK == 0 else T
    o, s = gdn_chunked(q, k, v, g, beta, s0.astype(jnp.float32), chunk)
    o = o * lax.rsqrt(jnp.mean(o * o, axis=-1, keepdims=True) + RMS_EPS) * norm_w.astype(jnp.float32)
    o = o * jax.nn.silu(z.astype(jnp.float32)).reshape(shp)
    return o.reshape(B, T, GDN_WIDTH).astype(qkv.dtype), new_buf, s


def memory_kv(mem, mem_norm_w, w_xkv):
    B, M, _ = mem.shape
    kv = rmsnorm(mem, mem_norm_w) @ w_xkv
    mk = kv[..., :D_MODEL].reshape(B, M, X_HEADS, X_HEAD_DIM)
    mv = kv[..., D_MODEL:].reshape(B, M, X_HEADS, X_HEAD_DIM)
    return mk, mv


def cross_attn(hn, mk, mv, w_xq, w_xo):
    B, T, _ = hn.shape
    q = (hn @ w_xq).reshape(B, T, X_HEADS, X_HEAD_DIM)
    s = jnp.einsum('bthd,bmhd->bhtm', q, mk.astype(q.dtype)).astype(jnp.float32) * (X_HEAD_DIM ** -0.5)
    p = jax.nn.softmax(s, axis=-1).astype(q.dtype)
    o = jnp.einsum('bhtm,bmhd->bthd', p, mv.astype(q.dtype)).reshape(B, T, X_HEADS * X_HEAD_DIM)
    return o @ w_xo


def swiglu(hn, w_gu, w_down):
    gu = hn @ w_gu
    return (jax.nn.silu(gu[..., :D_FF]) * gu[..., D_FF:]) @ w_down


def layer(h, mem_k, mem_v, conv_buf, s0, moba_fn, norm_mix_w, w_in, conv_w, a_log, dt_bias,
          gdn_norm_w, w_out, norm_x_w, w_xq, w_xo, norm_ffn_w, w_gu, w_down):
    B, T, _ = h.shape
    proj = rmsnorm(h, norm_mix_w) @ w_in
    mshp = (B, T, MOBA_HEADS, MOBA_HEAD_DIM)
    o0 = 3 * MOBA_WIDTH
    o1 = o0 + 3 * GDN_WIDTH
    o2 = o1 + GDN_WIDTH
    o3 = o2 + GDN_HEADS
    qm = proj[..., :MOBA_WIDTH].reshape(mshp)
    km = proj[..., MOBA_WIDTH:2 * MOBA_WIDTH].reshape(mshp)
    vm = proj[..., 2 * MOBA_WIDTH:o0].reshape(mshp)
    o_moba = moba_fn(qm, km, vm)
    o_gdn, new_buf, s_new = gdn_branch(proj[..., o0:o1], proj[..., o1:o2], proj[..., o2:o3],
                                       proj[..., o3:], conv_buf, s0, conv_w, a_log, dt_bias, gdn_norm_w)
    h = h + jnp.concatenate([o_moba.astype(h.dtype), o_gdn], axis=-1) @ w_out
    h = h + cross_attn(rmsnorm(h, norm_x_w), mem_k, mem_v, w_xq, w_xo)
    h = h + swiglu(rmsnorm(h, norm_ffn_w), w_gu, w_down)
    return h, km, vm, new_buf, s_new


def setup_inputs(seed: int = 0) -> dict:
    key = jax.random.key(seed)
    ks = jax.random.split(key, 28)
    f32 = jnp.float32

    def nrm(k, shape, scale):
        return jax.random.normal(k, shape, f32) * scale

    def gain(k, shape):
        return 1.0 + 0.02 * jax.random.normal(k, shape, f32)

    n_pages = PAST_LEN // PAGE_SIZE
    n_used = DEC_BATCH * n_pages
    n_phys = n_used + (n_used + 3) // 4
    page_table = jax.random.permutation(ks[0], n_phys)[:n_used].reshape(DEC_BATCH, n_pages).astype(jnp.int32)
    dt = jnp.exp(jax.random.uniform(ks[1], (DEPTH, GDN_HEADS), f32, math.log(1e-3), math.log(1e-1)))
    dt_bias = dt + jnp.log(-jnp.expm1(-dt))
    a_log = jnp.log(jax.random.uniform(ks[2], (DEPTH, GDN_HEADS), f32, 1.0, 16.0))
    xw = X_HEADS * X_HEAD_DIM
    return {
        'x_prompt': nrm(ks[3], (BATCH, SEQ, D_MODEL), 1.0),
        'x_sample': nrm(ks[4], (DEC_BATCH, DEC_SEQ, D_MODEL), 1.0),
        'cache_k': nrm(ks[5], (DEPTH, n_phys, PAGE_SIZE, MOBA_HEADS, MOBA_HEAD_DIM), 1.0),
        'cache_v': nrm(ks[6], (DEPTH, n_phys, PAGE_SIZE, MOBA_HEADS, MOBA_HEAD_DIM), 1.0),
        'page_table': page_table,
        'state_conv': nrm(ks[7], (DEPTH, DEC_BATCH, GDN_CONV - 1, 3 * GDN_WIDTH), 1.0),
        'state_gdn': nrm(ks[8], (DEPTH, DEC_BATCH, GDN_HEADS, GDN_HEAD_DIM, GDN_HEAD_DIM), 0.3),
        'cache_mem_k': nrm(ks[9], (DEPTH, DEC_BATCH, N_MEM, X_HEADS, X_HEAD_DIM), 1.0),
        'cache_mem_v': nrm(ks[10], (DEPTH, DEC_BATCH, N_MEM, X_HEADS, X_HEAD_DIM), 1.0),
        'mem_prompt': nrm(ks[11], (BATCH, N_MEM, D_MODEL), 1.0),
        'norm_mix_w': gain(ks[12], (DEPTH, D_MODEL)),
        'w_in': nrm(ks[13], (DEPTH, D_MODEL, IN_WIDTH), D_MODEL ** -0.5),
        'conv_w': nrm(ks[14], (DEPTH, GDN_CONV, 3 * GDN_WIDTH), GDN_CONV ** -0.5),
        'a_log': a_log,
        'dt_bias': dt_bias,
        'gdn_norm_w': gain(ks[15], (DEPTH, GDN_HEAD_DIM)),
        'w_out': nrm(ks[16], (DEPTH, MIX_WIDTH, D_MODEL), MIX_WIDTH ** -0.5),
        'norm_x_w': gain(ks[17], (DEPTH, D_MODEL)),
        'mem_norm_w': gain(ks[18], (DEPTH, D_MODEL)),
        'w_xq': nrm(ks[19], (DEPTH, D_MODEL, xw), D_MODEL ** -0.5),
        'w_xkv': nrm(ks[20], (DEPTH, D_MODEL, 2 * xw), D_MODEL ** -0.5),
        'w_xo': nrm(ks[21], (DEPTH, xw, D_MODEL), xw ** -0.5),
        'norm_ffn_w': gain(ks[22], (DEPTH, D_MODEL)),
        'w_gu': nrm(ks[23], (DEPTH, D_MODEL, 2 * D_FF), D_MODEL ** -0.5),
        'w_down': nrm(ks[24], (DEPTH, D_FF, D_MODEL), D_FF ** -0.5),
        'final_norm_w': gain(ks[25], (D_MODEL,)),
    }


def reference(x_prompt, x_sample, cache_k, cache_v, page_table, state_conv, state_gdn,
              cache_mem_k, cache_mem_v, mem_prompt, norm_mix_w, w_in, conv_w, a_log, dt_bias,
              gdn_norm_w, w_out, norm_x_w, mem_norm_w, w_xq, w_xkv, w_xo, norm_ffn_w, w_gu,
              w_down, final_norm_w):
    hp, hs = x_prompt, x_sample
    B = x_prompt.shape[0]
    kp_l, vp_l, bufp_l, sp_l, mkp_l, mvp_l = [], [], [], [], [], []
    ks_l, vs_l, bufs_l, ss_l = [], [], [], []
    for l in range(DEPTH):
        lw = (norm_mix_w[l], w_in[l], conv_w[l], a_log[l], dt_bias[l], gdn_norm_w[l], w_out[l],
              norm_x_w[l], w_xq[l], w_xo[l], norm_ffn_w[l], w_gu[l], w_down[l])
        mk_p, mv_p = memory_kv(mem_prompt, mem_norm_w[l], w_xkv[l])
        buf0 = jnp.zeros((B, GDN_CONV - 1, 3 * GDN_WIDTH), x_prompt.dtype)
        s0 = jnp.zeros((B, GDN_HEADS, GDN_HEAD_DIM, GDN_HEAD_DIM), jnp.float32)
        hp, kp, vp, bufp, sp = layer(hp, mk_p, mv_p, buf0, s0, moba_prompt, *lw)
        moba_s = functools.partial(moba_sample, pool_k=cache_k[l], pool_v=cache_v[l], page_table=page_table)
        hs, ksn, vsn, bufs, ss = layer(hs, cache_mem_k[l], cache_mem_v[l], state_conv[l], state_gdn[l], moba_s, *lw)
        kp_l.append(kp); vp_l.append(vp); bufp_l.append(bufp); sp_l.append(sp)
        mkp_l.append(mk_p); mvp_l.append(mv_p)
        ks_l.append(ksn); vs_l.append(vsn); bufs_l.append(bufs); ss_l.append(ss)
    y_prompt = rmsnorm(hp, final_norm_w)
    y_sample = rmsnorm(hs, final_norm_w)
    k_prompt = jnp.stack(kp_l)
    v_prompt = jnp.stack(vp_l)
    conv_prompt = jnp.stack(bufp_l)
    gdn_prompt = jnp.stack(sp_l)
    mem_k_prompt = jnp.stack(mkp_l)
    mem_v_prompt = jnp.stack(mvp_l)
    k_sample = jnp.stack(ks_l)
    v_sample = jnp.stack(vs_l)
    conv_sample = jnp.stack(bufs_l)
    gdn_sample = jnp.stack(ss_l)
    return (y_prompt, y_sample, k_prompt, v_prompt, conv_prompt, gdn_prompt, mem_k_prompt, mem_v_prompt,
            k_sample, v_sample, conv_sample, gdn_sample)
```

```python
import functools
import math

import jax
import jax.numpy as jnp
from jax import lax
from jax.experimental import pallas as pl
from jax.experimental.pallas import tpu as pltpu

F32 = jnp.float32
BF16 = jnp.bfloat16

MOBA_HEADS = 8
MOBA_HEAD_DIM = 64
MOBA_WIDTH = MOBA_HEADS * MOBA_HEAD_DIM
MOBA_BLOCK = 256
MOBA_TOPK = 3
GDN_HEAD_DIM = 128
GDN_HEADS = 4
GDN_WIDTH = GDN_HEADS * GDN_HEAD_DIM
GDN_CONV = 4
GDN_CHUNK = 64
X_HEADS = 4
RMS_EPS = 1e-6
L2_EPS = 1e-6

LANES = 128
SUBLANES = 8
VMEM_LIMIT = 56 * 1024 * 1024
NEG = -1e30


def _params(sem):
    return pltpu.CompilerParams(dimension_semantics=sem, vmem_limit_bytes=VMEM_LIMIT)


def _const_spec(shape):
    nd = len(shape)
    return pl.BlockSpec(shape, lambda *_: (0,) * nd, pipeline_mode=pl.Buffered(1))


def _dot(a, b):
    return jnp.dot(a, b, preferred_element_type=F32)


def _dot_nt(a, b):
    return lax.dot_general(a, b, (((1,), (1,)), ((), ())), preferred_element_type=F32)


def _dot_tn(a, b):
    return lax.dot_general(a, b, (((0,), (0,)), ((), ())), preferred_element_type=F32)


def _split2(x):
    hi = x.astype(BF16)
    lo = (x - hi.astype(F32)).astype(BF16)
    return hi, lo


def _split3(x):
    hi = x.astype(BF16)
    r = x - hi.astype(F32)
    mid = r.astype(BF16)
    lo = (r - mid.astype(F32)).astype(BF16)
    return hi, mid, lo


def _dot3(a, b, dot=_dot):
    ah, al = _split2(a)
    bh, bl = _split2(b)
    return dot(ah, bh) + (dot(ah, bl) + dot(al, bh))


def _dot_exact_lhs(a_bf16, b, dot=_dot):
    bh, bm, bl = _split3(b)
    return dot(a_bf16, bh) + (dot(a_bf16, bm) + dot(a_bf16, bl))


def _rms(x, w):
    return x * lax.rsqrt(jnp.mean(x * x, axis=-1, keepdims=True) + RMS_EPS) * w


def _sigmoid(x):
    return 1.0 / (1.0 + jnp.exp(-x))


def _silu(x):
    return x * _sigmoid(x)


def _softplus(x):
    return jnp.maximum(x, 0.0) + jnp.log1p(jnp.exp(-jnp.abs(x)))


def _norm_matmul_kernel(*refs, splits, n_hi, chunk):
    x_ref, nw_ref, w_ref = refs[:3]
    if n_hi:
        wlo_ref, out_refs = refs[3], refs[4:]
    else:
        out_refs = refs[3:]
    xn = _rms(x_ref[...], nw_ref[...])
    xh = xn.astype(BF16)
    if n_hi:
        xl = (xn - xh.astype(F32)).astype(BF16)
    c0 = 0
    for o_ref, width in zip(out_refs, splits):
        for s in range(0, width, chunk):
            wd = min(chunk, width - s)
            a, b = c0 + s, c0 + s + wd
            acc = _dot(xh, w_ref[:, a:b])
            if b <= n_hi:
                acc = acc + (_dot(xl, w_ref[:, a:b]) + _dot(xh, wlo_ref[:, a:b]))
            o_ref[:, s:s + wd] = acc
        c0 += width


def _norm_matmul(x, norm_w, w_hi, w_lo, splits, tm):
    m, k = x.shape
    n_hi = 0 if w_lo is None else w_lo.shape[1]
    assert m % tm == 0 and sum(splits) == w_hi.shape[1]
    in_specs = [pl.BlockSpec((tm, k), lambda i: (i, 0)), _const_spec((1, k)), _const_spec(w_hi.shape)]
    args = [x, norm_w.reshape(1, k), w_hi]
    if n_hi:
        in_specs.append(_const_spec(w_lo.shape))
        args.append(w_lo)
    return pl.pallas_call(
        functools.partial(_norm_matmul_kernel, splits=tuple(splits), n_hi=n_hi, chunk=512),
        grid=(m // tm,),
        in_specs=in_specs,
        out_specs=[pl.BlockSpec((tm, wd), lambda i: (i, 0)) for wd in splits],
        out_shape=[jax.ShapeDtypeStruct((m, wd), F32) for wd in splits],
        compiler_params=_params(("parallel",)),
        name="norm_matmul",
    )(*args)


def _topk_select(gate, valid, colf, ncol, axis):
    g = jnp.where(valid, gate, -jnp.inf)
    sel = None
    for _ in range(MOBA_TOPK):
        m = jnp.max(g, axis=axis, keepdims=True)
        idx = jnp.min(jnp.where(g == m, colf, float(ncol)), axis=axis, keepdims=True)
        pick = colf == idx
        sel = pick if sel is None else jnp.logical_or(sel, pick)
        g = jnp.where(pick, -jnp.inf, g)
    return jnp.logical_and(sel, valid)


def _moba_prompt_kernel(slope_ref, q_ref, k_ref, v_ref, o_ref, kmean_ref, *, blk, nb):
    hp = pl.program_id(1)
    i = pl.program_id(2)
    d = MOBA_HEAD_DIM

    @pl.when(i == 0)
    def _():
        for j in range(nb):
            kmean_ref[j:j + 1, :] = jnp.mean(k_ref[0, j * blk:(j + 1) * blk, :], axis=0, keepdims=True)

    q = q_ref[0]
    lane = lax.broadcasted_iota(jnp.int32, (blk, LANES), 1)
    colf = lax.broadcasted_iota(jnp.int32, (blk, nb), 1).astype(F32)
    valid = colf < i.astype(F32)
    rowq = lax.broadcasted_iota(jnp.int32, (blk, blk), 0)
    colk = lax.broadcasted_iota(jnp.int32, (blk, blk), 1)
    d0 = (rowq - colk).astype(F32)
    kmean = kmean_ref[...]
    outs = []
    for h2 in range(2):
        slope = slope_ref[2 * hp + h2]
        qh = jnp.where((lane >= d * h2) & (lane < d * (h2 + 1)), q, 0.0)
        gate = _dot3(qh, kmean, _dot_nt)
        self_f = _topk_select(gate, valid, colf, nb, 1).astype(F32)
        qb = (qh * (d ** -0.5)).astype(BF16)

        def scores(j):
            start = pl.multiple_of(j * blk, blk)
            kj = k_ref[0, pl.ds(start, blk), :].astype(BF16)
            vj = v_ref[0, pl.ds(start, blk), :].astype(BF16)
            return _dot_nt(qb, kj), vj

        s, vj = scores(i)
        s = jnp.where(d0 >= 0.0, s - slope * d0, NEG)
        m = jnp.max(s, axis=1, keepdims=True)
        p = jnp.exp(s - m)
        l = jnp.sum(p, axis=1, keepdims=True)
        acc = _dot(p.astype(BF16), vj)

        def body(j, carry):
            m, l, acc = carry
            s, vj = scores(j)
            off = ((i - j) * blk).astype(F32)
            selj = jnp.sum(jnp.where(colf == j.astype(F32), self_f, 0.0), axis=1, keepdims=True) > 0.5
            s = jnp.where(selj, s - slope * (d0 + off), NEG)
            m_new = jnp.maximum(m, jnp.max(s, axis=1, keepdims=True))
            alpha = jnp.exp(m - m_new)
            p = jnp.exp(s - m_new)
            l = alpha * l + jnp.sum(p, axis=1, keepdims=True)
            acc = alpha * acc + _dot(p.astype(BF16), vj)
            return m_new, l, acc

        m, l, acc = lax.fori_loop(0, i, body, (m, l, acc))
        outs.append(acc / l)
    o_ref[0] = jnp.where(lane < d, outs[0], outs[1])


def _moba_prompt(q, k, v, slopes):
    b, t, w = q.shape
    blk = MOBA_BLOCK
    assert t % blk == 0 and w % LANES == 0
    nb = t // blk
    kv_spec = pl.BlockSpec((1, t, LANES), lambda bi, hp, i: (bi, 0, hp))
    q_spec = pl.BlockSpec((1, blk, LANES), lambda bi, hp, i: (bi, i, hp))
    return pl.pallas_call(
        functools.partial(_moba_prompt_kernel, blk=blk, nb=nb),
        grid=(b, w // LANES, nb),
        in_specs=[pl.BlockSpec(memory_space=pltpu.SMEM), q_spec, kv_spec, kv_spec],
        out_specs=q_spec,
        out_shape=jax.ShapeDtypeStruct((b, t, w), F32),
        scratch_shapes=[pltpu.VMEM((nb, LANES), F32)],
        compiler_params=_params(("parallel", "parallel", "arbitrary")),
        name="moba_prompt",
    )(slopes, q, k, v)


def _tri_inverse(low, c):
    row = lax.broadcasted_iota(jnp.int32, (c, c), 0)
    col = lax.broadcasted_iota(jnp.int32, (c, c), 1)
    t = jnp.where(row == col, 1.0, 0.0) - low
    p = low
    for _ in range(int(math.log2(c)) - 1):
        p = _dot3(p, p)
        t = t + _dot3(t, p)
    return t


def _gdn_prompt_kernel(x_ref, z_ref, ab_ref, cw_ref, alog_ref, dtb_ref, nw_ref,
                       o_ref, s_out_ref, cb_out_ref,
                       xext_ref, s_ref, qn_ref, kn_ref, vv_ref, gc_ref, beta_ref, *, tt, c):
    t_idx = pl.program_id(1)
    nt = pl.num_programs(1)
    hd = GDN_HEAD_DIM
    gw = GDN_WIDTH
    halo = SUBLANES

    @pl.when(t_idx == 0)
    def _():
        s_ref[...] = jnp.zeros_like(s_ref)
        xext_ref[0:halo, :] = jnp.zeros((halo, 3 * gw), F32)

    xext_ref[halo:halo + tt, :] = x_ref[0]
    conv = xext_ref[halo:halo + tt, :] * cw_ref[GDN_CONV - 1:GDN_CONV, :]
    for j in range(GDN_CONV - 1):
        off = halo - (GDN_CONV - 1) + j
        conv = conv + xext_ref[off:off + tt, :] * cw_ref[j:j + 1, :]

    @pl.when(t_idx == nt - 1)
    def _():
        cb_out_ref[0] = xext_ref[halo + tt - (GDN_CONV - 1):halo + tt, :]

    xext_ref[0:halo, :] = xext_ref[tt:tt + halo, :]

    act = _silu(conv)
    for h in range(GDN_HEADS):
        qh = act[:, h * hd:(h + 1) * hd]
        kh = act[:, gw + h * hd:gw + (h + 1) * hd]
        qn_ref[:, h * hd:(h + 1) * hd] = qh * lax.rsqrt(jnp.sum(qh * qh, axis=-1, keepdims=True) + L2_EPS) * (hd ** -0.5)
        kn_ref[:, h * hd:(h + 1) * hd] = kh * lax.rsqrt(jnp.sum(kh * kh, axis=-1, keepdims=True) + L2_EPS)
    vv_ref[...] = act[:, 2 * gw:]

    abv = ab_ref[0]
    g_all = -jnp.exp(alog_ref[...]) * _softplus(abv + dtb_ref[...])
    beta_ref[...] = _sigmoid(abv)
    rc = lax.broadcasted_iota(jnp.int32, (c, c), 0)
    cc = lax.broadcasted_iota(jnp.int32, (c, c), 1)
    tri = jnp.where(rc >= cc, 1.0, 0.0).astype(BF16)
    for ci in range(tt // c):
        gc_ref[ci * c:(ci + 1) * c, :] = _dot_exact_lhs(tri, g_all[ci * c:(ci + 1) * c, :])

    incl = rc >= cc
    strict = rc > cc
    lane8 = lax.broadcasted_iota(jnp.int32, (SUBLANES, LANES), 1)
    nw = nw_ref[...]

    def chunk_body(ci, carry):
        r0 = pl.multiple_of(ci * c, c)
        gcs = gc_ref[pl.ds(r0, c), :]
        bet = beta_ref[pl.ds(r0, c), :]
        e_gc = jnp.exp(gcs)
        g_last = gcs[c - 1:c, :]
        e_rev = jnp.exp(g_last - gcs)
        e_last = jnp.exp(g_last)
        for h in range(GDN_HEADS):
            hs = slice(h * hd, (h + 1) * hd)
            qn = qn_ref[pl.ds(r0, c), hs]
            kn = kn_ref[pl.ds(r0, c), hs]
            vv = vv_ref[pl.ds(r0, c), hs]
            gcol = gcs[:, h:h + 1]
            onehot = jnp.where(lane8 == h, 1.0, 0.0).astype(BF16)
            grow = _dot_exact_lhs(onehot, gcs, _dot_nt)[0:1, :]
            bcol = bet[:, GDN_HEADS + h:GDN_HEADS + h + 1]
            decay = jnp.where(incl, jnp.exp(gcol - grow), 0.0)
            knb = kn.astype(BF16)
            kk = _dot_nt(knb, knb)
            low = jnp.where(strict, bcol * kk * decay, 0.0)
            tinv = _tri_inverse(low, c)
            tb = tinv.astype(BF16)
            value = _dot(tb, (vv * bcol).astype(BF16))
            kcd = _dot(tb, (kn * (bcol * e_gc[:, h:h + 1])).astype(BF16))
            aqk = _dot_nt(qn.astype(BF16), knb) * decay
            q_dec = qn * e_gc[:, h:h + 1]
            k_dec = kn * e_rev[:, h:h + 1]
            s = s_ref[h]
            sb = s.astype(BF16)
            v_new = value - _dot(kcd.astype(BF16), sb)
            o = _dot(q_dec.astype(BF16), sb) + _dot(aqk.astype(BF16), v_new.astype(BF16))
            s_ref[h] = s * e_last[:, h:h + 1] + _dot_tn(k_dec.astype(BF16), v_new.astype(BF16))
            o = o * lax.rsqrt(jnp.mean(o * o, axis=-1, keepdims=True) + RMS_EPS) * nw
            o_ref[0, pl.ds(r0, c), hs] = o * _silu(z_ref[0, pl.ds(r0, c), hs])
        return carry

    lax.fori_loop(0, tt // c, chunk_body, 0)

    @pl.when(t_idx == nt - 1)
    def _():
        s_out_ref[0] = s_ref[...]


def _gdn_prompt(gqkv, z, ab, conv_w, alog_pad, dtb_pad, norm_w, tt):
    b, t, w3 = gqkv.shape
    c = GDN_CHUNK
    assert t % tt == 0 and tt % c == 0 and w3 == 3 * GDN_WIDTH
    hd = GDN_HEAD_DIM
    tile = lambda wd: pl.BlockSpec((1, tt, wd), lambda bi, ti: (bi, ti, 0))
    return pl.pallas_call(
        functools.partial(_gdn_prompt_kernel, tt=tt, c=c),
        grid=(b, t // tt),
        in_specs=[tile(w3), tile(GDN_WIDTH), tile(LANES), _const_spec(conv_w.shape),
                  _const_spec((1, LANES)), _const_spec((1, LANES)), _const_spec((1, hd))],
        out_specs=[tile(GDN_WIDTH),
                   pl.BlockSpec((1, GDN_HEADS, hd, hd), lambda bi, ti: (bi, 0, 0, 0)),
                   pl.BlockSpec((1, GDN_CONV - 1, w3), lambda bi, ti: (bi, 0, 0))],
        out_shape=[jax.ShapeDtypeStruct((b, t, GDN_WIDTH), F32),
                   jax.ShapeDtypeStruct((b, GDN_HEADS, hd, hd), F32),
                   jax.ShapeDtypeStruct((b, GDN_CONV - 1, w3), F32)],
        scratch_shapes=[pltpu.VMEM((tt + SUBLANES, w3), F32),
                        pltpu.VMEM((GDN_HEADS, hd, hd), F32),
                        pltpu.VMEM((tt, GDN_WIDTH), F32),
                        pltpu.VMEM((tt, GDN_WIDTH), F32),
                        pltpu.VMEM((tt, GDN_WIDTH), F32),
                        pltpu.VMEM((tt, LANES), F32),
                        pltpu.VMEM((tt, LANES), F32)],
        compiler_params=_params(("parallel", "arbitrary")),
        name="gdn_prompt",
    )(gqkv, z, ab, conv_w, alog_pad, dtb_pad, norm_w.reshape(1, hd))


def _gdn_sample_kernel(x_ref, z_ref, ab_ref, cb_ref, s_in_ref, cw_ref, alog_ref, dtb_ref, nw_ref,
                       o_ref, s_out_ref, cb_out_ref):
    hd = GDN_HEAD_DIM
    gw = GDN_WIDTH
    nc = GDN_CONV - 1
    x = x_ref[0]
    cb = cb_ref[0]
    conv = x * cw_ref[nc:nc + 1, :]
    for j in range(nc):
        conv = conv + cb[j:j + 1, :] * cw_ref[j:j + 1, :]
    cb_out_ref[0, 0:nc - 1, :] = cb[1:nc, :]
    cb_out_ref[0, nc - 1:nc, :] = x
    act = _silu(conv)
    abv = ab_ref[0]
    g_all = -jnp.exp(alog_ref[...]) * _softplus(abv + dtb_ref[...])
    e_g = jnp.exp(g_all)
    beta = _sigmoid(abv)
    z = z_ref[0]
    nw = nw_ref[...]
    eye = (lax.broadcasted_iota(jnp.int32, (hd, hd), 0) == lax.broadcasted_iota(jnp.int32, (hd, hd), 1))

    def as_column(row):
        return jnp.sum(jnp.where(eye, row, 0.0), axis=1, keepdims=True)

    for h in range(GDN_HEADS):
        hs = slice(h * hd, (h + 1) * hd)
        qh = act[:, hs]
        kh = act[:, gw + h * hd:gw + (h + 1) * hd]
        vv = act[:, 2 * gw + h * hd:2 * gw + (h + 1) * hd]
        qn = qh * lax.rsqrt(jnp.sum(qh * qh, axis=-1, keepdims=True) + L2_EPS) * (hd ** -0.5)
        kn = kh * lax.rsqrt(jnp.sum(kh * kh, axis=-1, keepdims=True) + L2_EPS)
        eg = e_g[:, h:h + 1]
        bh = beta[:, GDN_HEADS + h:GDN_HEADS + h + 1]
        s = s_in_ref[0, h]
        k_col = as_column(kn)
        v_new = vv * bh - jnp.sum((k_col * (bh * eg)) * s, axis=0, keepdims=True)
        qk = jnp.sum(qn * kn, axis=-1, keepdims=True)
        o = jnp.sum((as_column(qn) * eg) * s, axis=0, keepdims=True) + qk * v_new
        s_out_ref[0, h] = s * eg + k_col * v_new
        o = o * lax.rsqrt(jnp.mean(o * o, axis=-1, keepdims=True) + RMS_EPS) * nw
        o_ref[0, :, hs] = o * _silu(z[:, hs])


def _gdn_sample(gqkv, z, ab, conv_buf, s0, conv_w, alog_pad, dtb_pad, norm_w):
    db, w3 = gqkv.shape
    hd = GDN_HEAD_DIM
    nc = GDN_CONV - 1
    row = lambda wd: pl.BlockSpec((1, 1, wd), lambda bi: (bi, 0, 0))
    st = pl.BlockSpec((1, GDN_HEADS, hd, hd), lambda bi: (bi, 0, 0, 0))
    cbs = pl.BlockSpec((1, nc, w3), lambda bi: (bi, 0, 0))
    o, s_new, cb_new = pl.pallas_call(
        _gdn_sample_kernel,
        grid=(db,),
        in_specs=[row(w3), row(GDN_WIDTH), row(LANES), cbs, st, _const_spec(conv_w.shape),
                  _const_spec((1, LANES)), _const_spec((1, LANES)), _const_spec((1, hd))],
        out_specs=[row(GDN_WIDTH), st, cbs],
        out_shape=[jax.ShapeDtypeStruct((db, 1, GDN_WIDTH), F32),
                   jax.ShapeDtypeStruct((db, GDN_HEADS, hd, hd), F32),
                   jax.ShapeDtypeStruct((db, nc, w3), F32)],
        compiler_params=_params(("parallel",)),
        name="gdn_sample",
    )(gqkv.reshape(db, 1, w3), z.reshape(db, 1, GDN_WIDTH), ab.reshape(db, 1, LANES), conv_buf, s0,
      conv_w, alog_pad, dtb_pad, norm_w.reshape(1, hd))
    return o.reshape(db, GDN_WIDTH), s_new, cb_new


def _page_sum_kernel(pt_ref, k_ref, o_ref, *, pages_per_block):
    p = pl.program_id(1)

    @pl.when(p == 0)
    def _():
        o_ref[...] = jnp.zeros_like(o_ref)

    blk = p // pages_per_block
    o_ref[0, pl.ds(blk, 1), :] += jnp.sum(k_ref[0], axis=0, keepdims=True)


def _page_sums(pool_k, page_table_flat, db, n_pages, pages_per_block):
    _, ps, w = pool_k.shape
    nblk = n_pages // pages_per_block
    return pl.pallas_call(
        functools.partial(_page_sum_kernel, pages_per_block=pages_per_block),
        grid_spec=pltpu.PrefetchScalarGridSpec(
            num_scalar_prefetch=1,
            grid=(db, n_pages),
            in_specs=[pl.BlockSpec((1, ps, w), lambda b, p, pt: (pt[b * n_pages + p], 0, 0))],
            out_specs=pl.BlockSpec((1, nblk, w), lambda b, p, pt: (b, 0, 0)),
        ),
        out_shape=jax.ShapeDtypeStruct((db, nblk, w), F32),
        compiler_params=_params(("parallel", "arbitrary")),
        name="moba_page_sums",
    )(page_table_flat, pool_k)


def _sample_select_kernel(q_ref, ksum_ref, sel_ref, *, nblk):
    d = MOBA_HEAD_DIM
    w = MOBA_WIDTH
    q = q_ref[0]
    kmean = ksum_ref[0] * (1.0 / MOBA_BLOCK)
    prod = kmean * q
    lane = lax.broadcasted_iota(jnp.int32, (w, LANES), 0)
    head = lax.broadcasted_iota(jnp.int32, (w, LANES), 1)
    seg = jnp.where((lane >= head * d) & (lane < (head + 1) * d), 1.0, 0.0).astype(BF16)
    gate = _dot_exact_lhs(seg, prod, lambda a, b: _dot(b, a))
    rowf = lax.broadcasted_iota(jnp.int32, (nblk, LANES), 0).astype(F32)
    sel = _topk_select(gate, rowf >= 0.0, rowf, nblk, 0)
    sel_ref[0] = sel.astype(F32)


def _sample_select(q, ksum):
    db, nblk, w = ksum.shape
    return pl.pallas_call(
        functools.partial(_sample_select_kernel, nblk=nblk),
        grid=(db,),
        in_specs=[pl.BlockSpec((1, 1, w), lambda b: (b, 0, 0)), pl.BlockSpec((1, nblk, w), lambda b: (b, 0, 0))],
        out_specs=pl.BlockSpec((1, nblk, LANES), lambda b: (b, 0, 0)),
        out_shape=jax.ShapeDtypeStruct((db, nblk, LANES), F32),
        compiler_params=_params(("parallel",)),
        name="moba_sample_select",
    )(q.reshape(db, 1, w), ksum)


def _moba_sample_attn_kernel(pg_ref, bk_ref, slope_ref, q_ref, kn_ref, vn_ref, k_ref, v_ref, o_ref,
                             m_ref, l_ref, acc_ref, *, n_slots, pages_per_block, page_size, past):
    b = pl.program_id(0)
    h = pl.program_id(1)
    sl = pl.program_id(2)
    d = MOBA_HEAD_DIM
    h2 = h % 2
    slope = slope_ref[h]
    lane = lax.broadcasted_iota(jnp.int32, (1, LANES), 1)
    hmask = (lane >= d * h2) & (lane < d * (h2 + 1))
    q = jnp.where(hmask, q_ref[0], 0.0) * (d ** -0.5)

    @pl.when(sl == 0)
    def _():
        s0 = jnp.sum(q * kn_ref[0], axis=-1, keepdims=True)
        m_ref[...] = s0
        l_ref[...] = jnp.ones_like(l_ref)
        acc_ref[...] = vn_ref[0]

    flat = (b * MOBA_HEADS + h) * n_slots + sl
    blk = bk_ref[flat // pages_per_block]
    half = sl % pages_per_block
    pos0 = blk * (pages_per_block * page_size) + half * page_size
    k = k_ref[0]
    s = jnp.sum(k * q, axis=-1, keepdims=True)
    r = lax.broadcasted_iota(jnp.int32, (page_size, 1), 0)
    s = s - slope * (past - pos0 - r).astype(F32)
    m_old = m_ref[...]
    m_new = jnp.maximum(m_old, jnp.max(s, axis=0, keepdims=True))
    alpha = jnp.exp(m_old - m_new)
    p = jnp.exp(s - m_new)
    l_ref[...] = alpha * l_ref[...] + jnp.sum(p, axis=0, keepdims=True)
    acc_ref[...] = alpha * acc_ref[...] + jnp.sum(p * v_ref[0], axis=0, keepdims=True)
    m_ref[...] = m_new

    @pl.when(sl == n_slots - 1)
    def _():
        o_ref[0] = acc_ref[...] / l_ref[...]


def _moba_sample_attn(q, k_new, v_new, pool_k, pool_v, pages_flat, blocks_flat, slopes, n_slots, pages_per_block, past):
    db, w = q.shape
    _, ps, _ = pool_k.shape
    row = pl.BlockSpec((1, 1, LANES), lambda b, h, s, pg, bk: (b, 0, h // 2))
    page = pl.BlockSpec((1, ps, LANES), lambda b, h, s, pg, bk: (pg[(b * MOBA_HEADS + h) * n_slots + s], 0, h // 2))
    out = pl.pallas_call(
        functools.partial(_moba_sample_attn_kernel, n_slots=n_slots, pages_per_block=pages_per_block,
                          page_size=ps, past=past),
        grid_spec=pltpu.PrefetchScalarGridSpec(
            num_scalar_prefetch=2,
            grid=(db, MOBA_HEADS, n_slots),
            in_specs=[pl.BlockSpec(memory_space=pltpu.SMEM), row, row, row, page, page],
            out_specs=pl.BlockSpec((1, 1, LANES), lambda b, h, s, pg, bk: (b * MOBA_HEADS + h, 0, 0)),
            scratch_shapes=[pltpu.VMEM((1, 1), F32), pltpu.VMEM((1, 1), F32), pltpu.VMEM((1, LANES), F32)],
        ),
        out_shape=jax.ShapeDtypeStruct((db * MOBA_HEADS, 1, LANES), F32),
        compiler_params=_params(("parallel", "parallel", "arbitrary")),
        name="moba_sample_attn",
    )(pages_flat, blocks_flat, slopes, q.reshape(db, 1, w), k_new.reshape(db, 1, w), v_new.reshape(db, 1, w),
      pool_k, pool_v)
    out = out.reshape(db, MOBA_HEADS // 2, 2, 2, MOBA_HEAD_DIM)
    return jnp.stack([out[:, :, 0, 0], out[:, :, 1, 1]], axis=2).reshape(db, w)


def _moba_sample(q, k_new, v_new, pool_k, pool_v, page_table, slopes):
    db, n_pages = page_table.shape
    n_phys, ps, hh, d = pool_k.shape
    w = hh * d
    assert MOBA_BLOCK % ps == 0
    ppb = MOBA_BLOCK // ps
    past = n_pages * ps
    assert past % MOBA_BLOCK == 0 and past // MOBA_BLOCK >= MOBA_TOPK
    pool_k = pool_k.reshape(n_phys, ps, w)
    pool_v = pool_v.reshape(n_phys, ps, w)
    pt_flat = page_table.reshape(-1)
    ksum = _page_sums(pool_k, pt_flat, db, n_pages, ppb)
    sel = _sample_select(q, ksum)
    nblk = n_pages // ppb
    selh = jnp.swapaxes(sel[:, :, :MOBA_HEADS], 1, 2) > 0.5
    order = jnp.argsort(jnp.where(selh, 0, 1), axis=-1, stable=True)[..., :MOBA_TOPK].astype(jnp.int32)
    pidx = (order[..., None] * ppb + jnp.arange(ppb, dtype=jnp.int32)).reshape(db, hh, MOBA_TOPK * ppb)
    pages = jnp.take_along_axis(page_table[:, None, :], pidx, axis=-1)
    return _moba_sample_attn(q, k_new, v_new, pool_k, pool_v, pages.reshape(-1), order.reshape(-1), slopes,
                             MOBA_TOPK * ppb, ppb, past)


def _mid_kernel(om_ref, og_ref, x_ref, wo_ref, nw_ref, wq_ref, h_ref, q_ref):
    half = om_ref.shape[1]
    h = x_ref[...] + (_dot(om_ref[...].astype(BF16), wo_ref[0:half, :])
                      + _dot(og_ref[...].astype(BF16), wo_ref[half:, :]))
    h_ref[...] = h
    q_ref[...] = _dot(_rms(h, nw_ref[...]).astype(BF16), wq_ref[...])


def _mid(o_moba, o_gdn, x, w_out, norm_w, w_xq, tm):
    m, dm = x.shape
    half = o_moba.shape[1]
    assert m % tm == 0
    tile = lambda wd: pl.BlockSpec((tm, wd), lambda i: (i, 0))
    return pl.pallas_call(
        _mid_kernel,
        grid=(m // tm,),
        in_specs=[tile(half), tile(o_gdn.shape[1]), tile(dm), _const_spec(w_out.shape), _const_spec((1, dm)),
                  _const_spec(w_xq.shape)],
        out_specs=[tile(dm), tile(w_xq.shape[1])],
        out_shape=[jax.ShapeDtypeStruct((m, dm), F32), jax.ShapeDtypeStruct((m, w_xq.shape[1]), F32)],
        compiler_params=_params(("parallel",)),
        name="out_proj_xq",
    )(o_moba, o_gdn, x, w_out, norm_w.reshape(1, dm), w_xq)


def _xattn_prompt_kernel(q_ref, mk_ref, mv_ref, o_ref, *, hd):
    for h in range(X_HEADS):
        hs = slice(h * hd, (h + 1) * hd)
        qh = (q_ref[0, :, hs] * (hd ** -0.5)).astype(BF16)
        s = _dot_nt(qh, mk_ref[0, :, hs].astype(BF16))
        m = jnp.max(s, axis=1, keepdims=True)
        p = jnp.exp(s - m)
        l = jnp.sum(p, axis=1, keepdims=True)
        o_ref[0, :, hs] = _dot(p.astype(BF16), mv_ref[0, :, hs].astype(BF16)) / l


def _xattn_prompt(qx, mk, mv, tm):
    b, t, w = qx.shape
    nm = mk.shape[1]
    assert t % tm == 0
    q_spec = pl.BlockSpec((1, tm, w), lambda bi, ti: (bi, ti, 0))
    m_spec = pl.BlockSpec((1, nm, w), lambda bi, ti: (bi, 0, 0))
    return pl.pallas_call(
        functools.partial(_xattn_prompt_kernel, hd=w // X_HEADS),
        grid=(b, t // tm),
        in_specs=[q_spec, m_spec, m_spec],
        out_specs=q_spec,
        out_shape=jax.ShapeDtypeStruct((b, t, w), F32),
        compiler_params=_params(("parallel", "parallel")),
        name="xattn_prompt",
    )(qx, mk, mv)


def _xattn_sample_kernel(q_ref, mk_ref, mv_ref, o_ref, *, hd):
    for h in range(X_HEADS):
        hs = slice(h * hd, (h + 1) * hd)
        qh = q_ref[0, :, hs] * (hd ** -0.5)
        s = jnp.sum(mk_ref[0, :, hs] * qh, axis=-1, keepdims=True)
        m = jnp.max(s, axis=0, keepdims=True)
        p = jnp.exp(s - m)
        l = jnp.sum(p, axis=0, keepdims=True)
        o_ref[0, :, hs] = jnp.sum(p * mv_ref[0, :, hs], axis=0, keepdims=True) / l


def _xattn_sample(qx, mk, mv):
    db, w = qx.shape
    nm = mk.shape[1]
    q_spec = pl.BlockSpec((1, 1, w), lambda bi: (bi, 0, 0))
    m_spec = pl.BlockSpec((1, nm, w), lambda bi: (bi, 0, 0))
    return pl.pallas_call(
        functools.partial(_xattn_sample_kernel, hd=w // X_HEADS),
        grid=(db,),
        in_specs=[q_spec, m_spec, m_spec],
        out_specs=q_spec,
        out_shape=jax.ShapeDtypeStruct((db, 1, w), F32),
        compiler_params=_params(("parallel",)),
        name="xattn_sample",
    )(qx.reshape(db, 1, w), mk, mv).reshape(db, w)


def _tail_kernel(h_ref, ox_ref, wxo_ref, nf_ref, wgu_ref, wd_ref, fn_ref, y_ref, *, d_ff, chunk):
    h = h_ref[...] + _dot(ox_ref[...].astype(BF16), wxo_ref[...])
    hn = _rms(h, nf_ref[...]).astype(BF16)
    acc = jnp.zeros(h.shape, F32)
    for c0 in range(0, d_ff, chunk):
        g = _dot(hn, wgu_ref[:, c0:c0 + chunk])
        u = _dot(hn, wgu_ref[:, d_ff + c0:d_ff + c0 + chunk])
        acc = acc + _dot((_silu(g) * u).astype(BF16), wd_ref[c0:c0 + chunk, :])
    y_ref[...] = _rms(h + acc, fn_ref[...])


def _tail(h, ox, w_xo, norm_ffn_w, w_gu, w_down, final_norm_w, tm):
    m, dm = h.shape
    d_ff = w_down.shape[0]
    chunk = 2 * LANES
    assert m % tm == 0 and d_ff % chunk == 0
    tile = lambda wd: pl.BlockSpec((tm, wd), lambda i: (i, 0))
    return pl.pallas_call(
        functools.partial(_tail_kernel, d_ff=d_ff, chunk=chunk),
        grid=(m // tm,),
        in_specs=[tile(dm), tile(ox.shape[1]), _const_spec(w_xo.shape), _const_spec((1, dm)),
                  _const_spec(w_gu.shape), _const_spec(w_down.shape), _const_spec((1, dm))],
        out_specs=tile(dm),
        out_shape=jax.ShapeDtypeStruct((m, dm), F32),
        compiler_params=_params(("parallel",)),
        name="xo_swiglu_norm",
    )(h, ox, w_xo, norm_ffn_w.reshape(1, dm), w_gu, w_down, final_norm_w.reshape(1, dm))


def _row_tile(m, pref):
    return pref if m % pref == 0 else m


def kernel(x_prompt, x_sample, cache_k, cache_v, page_table, state_conv, state_gdn, cache_mem_k, cache_mem_v, mem_prompt, norm_mix_w, w_in, conv_w, a_log, dt_bias, gdn_norm_w, w_out, norm_x_w, mem_norm_w, w_xq, w_xkv, w_xo, norm_ffn_w, w_gu, w_down, final_norm_w):
    assert w_in.shape[0] == 1, "one layer"
    b, t, dm = x_prompt.shape
    db, ds, _ = x_sample.shape
    assert ds == 1
    mw, gw = MOBA_WIDTH, GDN_WIDTH
    n_in = 3 * mw + 4 * gw + 2 * GDN_HEADS
    assert w_in.shape[2] == n_in

    w_in_p = jnp.pad(w_in[0], ((0, 0), (0, 3 * mw + 4 * gw + LANES - n_in)))
    w_in_hi = w_in_p.astype(BF16)
    w_in_lo = (w_in_p[:, :2 * mw] - w_in_hi[:, :2 * mw].astype(F32)).astype(BF16)
    in_splits = (mw, mw, mw, 3 * gw, gw, LANES)
    w_out_b, w_xq_b, w_xo_b = w_out[0].astype(BF16), w_xq[0].astype(BF16), w_xo[0].astype(BF16)
    w_gu_b, w_down_b, w_xkv_b = w_gu[0].astype(BF16), w_down[0].astype(BF16), w_xkv[0].astype(BF16)
    xw = w_xq.shape[2]
    pad_h = lambda v: jnp.pad(v.astype(F32), (0, LANES - GDN_HEADS)).reshape(1, LANES)
    alog_pad, dtb_pad = pad_h(a_log[0]), pad_h(dt_bias[0])
    slopes = jnp.exp2(-8.0 * jnp.arange(1, MOBA_HEADS + 1, dtype=F32) / MOBA_HEADS)

    mp = b * t
    xp = x_prompt.reshape(mp, dm)
    qm, km, vm, gqkv, z, ab = _norm_matmul(xp, norm_mix_w[0], w_in_hi, w_in_lo, in_splits, _row_tile(mp, 512))
    mk_p, mv_p = _norm_matmul(mem_prompt.reshape(-1, dm), mem_norm_w[0], w_xkv_b, None, (xw, xw),
                              _row_tile(mem_prompt.shape[0] * mem_prompt.shape[1], 512))
    n_mem = mem_prompt.shape[1]
    o_moba = _moba_prompt(qm.reshape(b, t, mw), km.reshape(b, t, mw), vm.reshape(b, t, mw), slopes)
    o_gdn, s_p, cb_p = _gdn_prompt(gqkv.reshape(b, t, 3 * gw), z.reshape(b, t, gw), ab.reshape(b, t, LANES),
                                   conv_w[0], alog_pad, dtb_pad, gdn_norm_w[0], _row_tile(t, 512))
    h1, qx = _mid(o_moba.reshape(mp, mw), o_gdn.reshape(mp, gw), xp, w_out_b, norm_x_w[0], w_xq_b, _row_tile(mp, 512))
    ox = _xattn_prompt(qx.reshape(b, t, xw), mk_p.reshape(b, n_mem, xw), mv_p.reshape(b, n_mem, xw), _row_tile(t, 512))
    y_p = _tail(h1, ox.reshape(mp, xw), w_xo_b, norm_ffn_w[0], w_gu_b, w_down_b, final_norm_w, _row_tile(mp, 256))

    xs = x_sample.reshape(db, dm)
    qs, ks, vs, gqkv_s, z_s, ab_s = _norm_matmul(xs, norm_mix_w[0], w_in_hi, w_in_lo, in_splits, db)
    o_moba_s = _moba_sample(qs, ks, vs, cache_k[0], cache_v[0], page_table, slopes)
    o_gdn_s, s_s, cb_s = _gdn_sample(gqkv_s, z_s, ab_s, state_conv[0], state_gdn[0], conv_w[0], alog_pad, dtb_pad,
                                     gdn_norm_w[0])
    h1_s, qx_s = _mid(o_moba_s, o_gdn_s, xs, w_out_b, norm_x_w[0], w_xq_b, db)
    ox_s = _xattn_sample(qx_s, cache_mem_k[0].reshape(db, -1, xw), cache_mem_v[0].reshape(db, -1, xw))
    y_s = _tail(h1_s, ox_s, w_xo_b, norm_ffn_w[0], w_gu_b, w_down_b, final_norm_w, db)

    hh, hd = MOBA_HEADS, MOBA_HEAD_DIM
    xh = X_HEADS
    return (y_p.reshape(b, t, dm), y_s.reshape(db, 1, dm),
            km.reshape(1, b, t, hh, hd), vm.reshape(1, b, t, hh, hd),
            cb_p[None], s_p[None],
            mk_p.reshape(1, b, n_mem, xh, xw // xh), mv_p.reshape(1, b, n_mem, xh, xw // xh),
            ks.reshape(1, db, 1, hh, hd), vs.reshape(1, db, 1, hh, hd),
            cb_s[None], s_s[None])
```

```python
import functools
import math

import jax
import jax.numpy as jnp
from jax import lax
from jax.experimental import pallas as pl
from jax.experimental.pallas import tpu as pltpu

F32 = jnp.float32
BF16 = jnp.bfloat16

MOBA_HEADS = 8
MOBA_HEAD_DIM = 64
MOBA_WIDTH = MOBA_HEADS * MOBA_HEAD_DIM
MOBA_BLOCK = 256
MOBA_TOPK = 3
GDN_HEAD_DIM = 128
GDN_HEADS = 4
GDN_WIDTH = GDN_HEADS * GDN_HEAD_DIM
GDN_CONV = 4
GDN_CHUNK = 64
GDN_GROUP = 4
X_HEADS = 4
RMS_EPS = 1e-6
L2_EPS = 1e-6

LANES = 128
SUBLANES = 8
VMEM_LIMIT = 56 * 1024 * 1024
NEG = -1e30


def _params(sem):
    return pltpu.CompilerParams(dimension_semantics=sem, vmem_limit_bytes=VMEM_LIMIT)


def _const_spec(shape):
    nd = len(shape)
    return pl.BlockSpec(shape, lambda *_: (0,) * nd, pipeline_mode=pl.Buffered(1))


def _dot(a, b):
    return jnp.dot(a, b, preferred_element_type=F32)


def _dot_nt(a, b):
    return lax.dot_general(a, b, (((1,), (1,)), ((), ())), preferred_element_type=F32)


def _dot_tn(a, b):
    return lax.dot_general(a, b, (((0,), (0,)), ((), ())), preferred_element_type=F32)


def _split2(x):
    hi = x.astype(BF16)
    lo = (x - hi.astype(F32)).astype(BF16)
    return hi, lo


def _split3(x):
    hi = x.astype(BF16)
    r = x - hi.astype(F32)
    mid = r.astype(BF16)
    lo = (r - mid.astype(F32)).astype(BF16)
    return hi, mid, lo


def _dot3(a, b, dot=_dot):
    ah, al = _split2(a)
    bh, bl = _split2(b)
    return dot(ah, bh) + (dot(ah, bl) + dot(al, bh))


def _dot_exact_lhs(a_bf16, b, dot=_dot):
    bh, bm, bl = _split3(b)
    return dot(a_bf16, bh) + (dot(a_bf16, bm) + dot(a_bf16, bl))


def _rms(x, w):
    return x * lax.rsqrt(jnp.mean(x * x, axis=-1, keepdims=True) + RMS_EPS) * w


def _sigmoid(x):
    return 1.0 / (1.0 + jnp.exp(-x))


def _silu(x):
    return x * _sigmoid(x)


def _softplus(x):
    return jnp.maximum(x, 0.0) + jnp.log1p(jnp.exp(-jnp.abs(x)))


def _norm_matmul_kernel(*refs, splits, n_hi, chunk):
    x_ref, nw_ref, w_ref = refs[:3]
    if n_hi:
        wlo_ref, out_refs = refs[3], refs[4:]
    else:
        out_refs = refs[3:]
    xn = _rms(x_ref[...], nw_ref[...])
    xh = xn.astype(BF16)
    if n_hi:
        xl = (xn - xh.astype(F32)).astype(BF16)
    c0 = 0
    for o_ref, width in zip(out_refs, splits):
        for s in range(0, width, chunk):
            wd = min(chunk, width - s)
            a, b = c0 + s, c0 + s + wd
            acc = _dot(xh, w_ref[:, a:b])
            if b <= n_hi:
                acc = acc + (_dot(xl, w_ref[:, a:b]) + _dot(xh, wlo_ref[:, a:b]))
            o_ref[:, s:s + wd] = acc
        c0 += width


def _norm_matmul(x, norm_w, w_hi, w_lo, splits, tm):
    m, k = x.shape
    n_hi = 0 if w_lo is None else w_lo.shape[1]
    assert m % tm == 0 and sum(splits) == w_hi.shape[1]
    in_specs = [pl.BlockSpec((tm, k), lambda i: (i, 0)), _const_spec((1, k)), _const_spec(w_hi.shape)]
    args = [x, norm_w.reshape(1, k), w_hi]
    if n_hi:
        in_specs.append(_const_spec(w_lo.shape))
        args.append(w_lo)
    return pl.pallas_call(
        functools.partial(_norm_matmul_kernel, splits=tuple(splits), n_hi=n_hi, chunk=512),
        grid=(m // tm,),
        in_specs=in_specs,
        out_specs=[pl.BlockSpec((tm, wd), lambda i: (i, 0)) for wd in splits],
        out_shape=[jax.ShapeDtypeStruct((m, wd), F32) for wd in splits],
        compiler_params=_params(("parallel",)),
        name="norm_matmul",
    )(*args)


def _topk_select(gate, valid, colf, ncol, axis):
    g = jnp.where(valid, gate, -jnp.inf)
    sel = None
    for _ in range(MOBA_TOPK):
        m = jnp.max(g, axis=axis, keepdims=True)
        idx = jnp.min(jnp.where(g == m, colf, float(ncol)), axis=axis, keepdims=True)
        pick = colf == idx
        sel = pick if sel is None else jnp.logical_or(sel, pick)
        g = jnp.where(pick, -jnp.inf, g)
    return jnp.logical_and(sel, valid)


AUG_ALIBI = 9
AUG_SEL0 = 16
V_ROWS = LANES + 16
LOG2E = 1.4426950408889634
MOBA_UNROLL = 4


def _moba_prompt_kernel(coef_ref, q_ref, k_ref, v_ref, o_ref, kaug_ref, vt_ref, kmp_ref, qat_ref, acc_ref,
                        st_ref, p_ref, al_ref, *, blk, nb):
    hp = pl.program_id(1)
    i = pl.program_id(2)
    d = MOBA_HEAD_DIM
    i_f = jnp.asarray(i, F32)

    @pl.when(i == 0)
    def _():
        kmp_ref[...] = jnp.zeros_like(kmp_ref)
        lane = lax.broadcasted_iota(jnp.int32, (blk, LANES), 1)
        rowf = lax.broadcasted_iota(jnp.int32, (blk, LANES), 0).astype(F32)

        def prep(j, carry):
            start = pl.multiple_of(j * blk, blk)
            kj = k_ref[0, pl.ds(start, blk), :]
            kmp_ref[pl.ds(AUG_SEL0 + j, 1), :] = jnp.mean(kj, axis=0, keepdims=True)
            j_f = jnp.asarray(j, F32)
            aug = jnp.where(lane < 3, j_f, jnp.where(lane < 6, 1.0, jnp.where(lane < AUG_ALIBI, rowf, 0.0)))
            aug = jnp.where(lane == AUG_SEL0 + j, 1.0, aug)
            kaug_ref[j, :, 0:LANES] = kj.astype(BF16)
            kaug_ref[j, :, LANES:2 * LANES] = aug.astype(BF16)
            vt_ref[j, 0:LANES, :] = v_ref[0, pl.ds(start, blk), :].T.astype(BF16)
            vt_ref[j, LANES:V_ROWS, :] = jnp.ones((V_ROWS - LANES, blk), BF16)
            return carry

        lax.fori_loop(0, nb, prep, 0)

    qt = q_ref[0].T
    row = lax.broadcasted_iota(jnp.int32, (LANES, blk), 0)
    rowf = row.astype(F32)
    valid = (row >= AUG_SEL0) & (row < AUG_SEL0 + i)
    in_sel = (row >= AUG_SEL0) & (row < AUG_SEL0 + nb)
    kmp = kmp_ref[...]
    for h2 in range(2):
        head = 2 * hp + h2
        qh = jnp.where((row >= d * h2) & (row < d * (h2 + 1)), qt, 0.0)
        gate = _dot3(kmp, qh)
        sel = _topk_select(gate, valid, rowf, LANES, 0)
        aug = jnp.where(in_sel, jnp.where(jnp.logical_or(sel, row == AUG_SEL0 + i), 0.0, NEG), 0.0)
        c_blk = jnp.full((1, blk), -(coef_ref[head, 6] * i_f), F32)
        for li, part in zip((3, 4, 5), _split3(c_blk)):
            aug = jnp.where(row == li, part.astype(F32), aug)
        for li, ci in ((0, 0), (1, 1), (2, 2), (6, 3), (7, 4), (8, 5)):
            aug = jnp.where(row == li, coef_ref[head, ci], aug)
        qat_ref[h2, 0:LANES, :] = (qh * (d ** -0.5 * LOG2E)).astype(BF16)
        qat_ref[h2, LANES:2 * LANES, :] = aug.astype(BF16)

    def stage_a(slot, kb):
        for h2 in range(2):
            st_ref[slot, h2] = _dot(kaug_ref[kb], qat_ref[h2])

    def stage_b(slot, ms):
        new = []
        for h2 in range(2):
            st = st_ref[slot, h2]
            m_new = jnp.maximum(ms[h2], jnp.max(st, axis=0, keepdims=True))
            al_ref[slot, h2] = jnp.exp2(ms[h2] - m_new)
            p_ref[slot, h2] = jnp.exp2(st - m_new).astype(BF16)
            new.append(m_new)
        return tuple(new)

    def stage_c(slot, kb):
        for h2 in range(2):
            acc_ref[h2] = al_ref[slot, h2] * acc_ref[h2] + _dot(vt_ref[kb], p_ref[slot, h2])

    acc_ref[...] = jnp.zeros_like(acc_ref)
    p_ref[1] = jnp.zeros(p_ref.shape[1:], BF16)
    al_ref[1] = jnp.ones(al_ref.shape[1:], F32)
    keyi = lax.broadcasted_iota(jnp.int32, (blk, blk), 0)
    qryi = lax.broadcasted_iota(jnp.int32, (blk, blk), 1)
    for h2 in range(2):
        st_ref[0, h2] = jnp.where(keyi <= qryi, _dot(kaug_ref[i], qat_ref[h2]), NEG)
    ms = (jnp.full((1, blk), NEG, F32),) * 2

    def step(s, slot, ms):
        stage_c(slot, jnp.where(s == 2, i, jnp.maximum(s - 3, 0)))
        ms = stage_b(1 - slot, ms)
        stage_a(slot, s - 1)
        return ms

    def unrolled(t, ms):
        for u in range(MOBA_UNROLL):
            ms = step(MOBA_UNROLL * t + 1 + u, (1 + u) % 2, ms)
        return ms

    n_full = i // MOBA_UNROLL
    ms = lax.fori_loop(0, n_full, unrolled, ms)
    ms = lax.fori_loop(MOBA_UNROLL * n_full + 1, i + 1, lambda s, ms: step(s, s & 1, ms), ms)
    last = i & 1
    stage_c(1 - last, jnp.where(i == 1, i, jnp.maximum(i - 2, 0)))
    stage_b(last, ms)
    stage_c(last, jnp.maximum(i - 1, 0))

    outs = []
    for h2 in range(2):
        acc = acc_ref[h2]
        outs.append(acc[0:LANES, :] / acc[LANES:LANES + 1, :])
    o_ref[0] = jnp.where(row < d, outs[0], outs[1]).T


def _moba_prompt(q, k, v, slopes):
    b, t, w = q.shape
    blk = MOBA_BLOCK
    assert t % blk == 0 and w == MOBA_WIDTH and blk == 2 * LANES
    nb = t // blk
    assert AUG_SEL0 + nb <= LANES and nb <= 256 and AUG_ALIBI <= AUG_SEL0
    c1 = slopes * LOG2E
    c256 = c1 * blk
    coef = jnp.stack([p.astype(F32) for p in _split3(c256)] + [p.astype(F32) for p in _split3(c1)]
                     + [c256, jnp.zeros_like(c1)], axis=1)
    kv_spec = pl.BlockSpec((1, t, LANES), lambda bi, hp, i: (bi, 0, hp))
    q_spec = pl.BlockSpec((1, blk, LANES), lambda bi, hp, i: (bi, i, hp))
    return pl.pallas_call(
        functools.partial(_moba_prompt_kernel, blk=blk, nb=nb),
        grid=(b, w // LANES, nb),
        in_specs=[pl.BlockSpec(memory_space=pltpu.SMEM), q_spec, kv_spec, kv_spec],
        out_specs=q_spec,
        out_shape=jax.ShapeDtypeStruct((b, t, w), F32),
        scratch_shapes=[pltpu.VMEM((nb, blk, 2 * LANES), BF16),
                        pltpu.VMEM((nb, V_ROWS, blk), BF16),
                        pltpu.VMEM((LANES, LANES), F32),
                        pltpu.VMEM((2, 2 * LANES, blk), BF16),
                        pltpu.VMEM((2, V_ROWS, blk), F32),
                        pltpu.VMEM((2, 2, blk, blk), F32),
                        pltpu.VMEM((2, 2, blk, blk), BF16),
                        pltpu.VMEM((2, 2, 1, blk), F32)],
        compiler_params=_params(("parallel", "parallel", "arbitrary")),
        name="moba_prompt",
    )(coef, q, k, v)


def _tri_inverses(lows, c):
    row = lax.broadcasted_iota(jnp.int32, (c, c), 0)
    col = lax.broadcasted_iota(jnp.int32, (c, c), 1)
    eye = jnp.where(row == col, 1.0, 0.0)
    ts = [eye - low for low in lows]
    ps = [low.astype(BF16) for low in lows]
    for _ in range(int(math.log2(c)) - 1):
        ps = [_dot(p, p).astype(BF16) for p in ps]
        ts = [t + _dot(t.astype(BF16), p) for t, p in zip(ts, ps)]
    return ts


def _gdn_prompt_kernel(x_ref, z_ref, ab_ref, cw_ref, alog_ref, dtb_ref, nw_ref,
                       o_ref, s_out_ref, cb_out_ref,
                       xext_ref, s_ref, qn_ref, kn_ref, vv_ref, gc_ref, beta_ref,
                       el_ref, m_ref, n_ref, qe_ref, o0_ref, *, tt, c):
    t_idx = pl.program_id(1)
    nt = pl.num_programs(1)
    hd = GDN_HEAD_DIM
    gw = GDN_WIDTH
    halo = SUBLANES

    @pl.when(t_idx == 0)
    def _():
        s_ref[...] = jnp.zeros_like(s_ref)
        xext_ref[0:halo, :] = jnp.zeros((halo, 3 * gw), F32)

    xext_ref[halo:halo + tt, :] = x_ref[0]
    conv = xext_ref[halo:halo + tt, :] * cw_ref[GDN_CONV - 1:GDN_CONV, :]
    for j in range(GDN_CONV - 1):
        off = halo - (GDN_CONV - 1) + j
        conv = conv + xext_ref[off:off + tt, :] * cw_ref[j:j + 1, :]

    @pl.when(t_idx == nt - 1)
    def _():
        cb_out_ref[0] = xext_ref[halo + tt - (GDN_CONV - 1):halo + tt, :]

    xext_ref[0:halo, :] = xext_ref[tt:tt + halo, :]

    act = _silu(conv)
    for h in range(GDN_HEADS):
        qh = act[:, h * hd:(h + 1) * hd]
        kh = act[:, gw + h * hd:gw + (h + 1) * hd]
        qn_ref[:, h * hd:(h + 1) * hd] = qh * lax.rsqrt(jnp.sum(qh * qh, axis=-1, keepdims=True) + L2_EPS) * (hd ** -0.5)
        kn_ref[:, h * hd:(h + 1) * hd] = kh * lax.rsqrt(jnp.sum(kh * kh, axis=-1, keepdims=True) + L2_EPS)
    vv_ref[...] = act[:, 2 * gw:]

    abv = ab_ref[0]
    g_all = -jnp.exp(alog_ref[...]) * _softplus(abv + dtb_ref[...])
    beta_ref[...] = _sigmoid(abv)
    rc = lax.broadcasted_iota(jnp.int32, (c, c), 0)
    cc = lax.broadcasted_iota(jnp.int32, (c, c), 1)
    tri = jnp.where(rc >= cc, 1.0, 0.0).astype(BF16)
    for ci in range(tt // c):
        gc_ref[ci * c:(ci + 1) * c, :] = _dot_exact_lhs(tri, g_all[ci * c:(ci + 1) * c, :])

    incl = rc >= cc
    strict = rc > cc
    lane8 = lax.broadcasted_iota(jnp.int32, (SUBLANES, LANES), 1)
    nw = nw_ref[...]

    def prepare_group(gi, carry):
        probs = []
        for cj in range(GDN_GROUP):
            ci = gi * GDN_GROUP + cj
            r0 = pl.multiple_of(ci * c, c)
            gcs = gc_ref[pl.ds(r0, c), :]
            bet = beta_ref[pl.ds(r0, c), :]
            e_gc = jnp.exp(gcs)
            g_last = gcs[c - 1:c, :]
            e_rev = jnp.exp(g_last - gcs)
            el_ref[pl.ds(ci, 1), :] = jnp.exp(g_last)
            for h in range(GDN_HEADS):
                hs = slice(h * hd, (h + 1) * hd)
                onehot = jnp.where(lane8 == h, 1.0, 0.0).astype(BF16)
                grow = _dot_exact_lhs(onehot, gcs, _dot_nt)[0:1, :]
                probs.append(dict(
                    idx=ci * GDN_HEADS + h, qn=qn_ref[pl.ds(r0, c), hs], kn=kn_ref[pl.ds(r0, c), hs],
                    vv=vv_ref[pl.ds(r0, c), hs], bcol=bet[:, GDN_HEADS + h:GDN_HEADS + h + 1],
                    egc=e_gc[:, h:h + 1], erev=e_rev[:, h:h + 1],
                    decay=jnp.where(incl, jnp.exp(gcs[:, h:h + 1] - grow), 0.0)))
        for p in probs:
            p["knb"] = p["kn"].astype(BF16)
        lows = [jnp.where(strict, p["bcol"] * _dot_nt(p["knb"], p["knb"]) * p["decay"], 0.0) for p in probs]
        tinvs = _tri_inverses(lows, c)
        sols = []
        for p, tinv in zip(probs, tinvs):
            rhs = jnp.concatenate([p["kn"] * (p["bcol"] * p["egc"]), p["vv"] * p["bcol"]], axis=1)
            sols.append(_dot(tinv.astype(BF16), rhs.astype(BF16)).astype(BF16))
        for p, sol in zip(probs, sols):
            aqk = (_dot_nt(p["qn"].astype(BF16), p["knb"]) * p["decay"]).astype(BF16)
            upd = _dot_tn((p["kn"] * p["erev"]).astype(BF16), sol)
            out = _dot(aqk, sol)
            m_ref[p["idx"]] = upd[:, 0:hd].astype(BF16)
            n_ref[p["idx"]] = upd[:, hd:]
            qe_ref[p["idx"]] = (p["qn"] * p["egc"] - out[:, 0:hd]).astype(BF16)
            o0_ref[p["idx"]] = out[:, hd:]
        return carry

    lax.fori_loop(0, tt // (GDN_GROUP * c), prepare_group, 0)

    def advance(ci, carry):
        r0 = pl.multiple_of(ci * c, c)
        e_last = el_ref[pl.ds(ci, 1), :]
        for h in range(GDN_HEADS):
            hs = slice(h * hd, (h + 1) * hd)
            idx = ci * GDN_HEADS + h
            s = s_ref[h]
            sb = s.astype(BF16)
            o = _dot(qe_ref[idx], sb) + o0_ref[idx]
            s_ref[h] = s * e_last[:, h:h + 1] - _dot(m_ref[idx], sb) + n_ref[idx]
            o = o * lax.rsqrt(jnp.mean(o * o, axis=-1, keepdims=True) + RMS_EPS) * nw
            o_ref[0, pl.ds(r0, c), hs] = o * _silu(z_ref[0, pl.ds(r0, c), hs])
        return carry

    lax.fori_loop(0, tt // c, advance, 0)

    @pl.when(t_idx == nt - 1)
    def _():
        s_out_ref[0] = s_ref[...]


def _gdn_prompt(gqkv, z, ab, conv_w, alog_pad, dtb_pad, norm_w, tt):
    b, t, w3 = gqkv.shape
    c = GDN_CHUNK
    assert t % tt == 0 and tt % (GDN_GROUP * c) == 0 and w3 == 3 * GDN_WIDTH
    hd = GDN_HEAD_DIM
    nch = tt // c
    tile = lambda wd: pl.BlockSpec((1, tt, wd), lambda bi, ti: (bi, ti, 0))
    return pl.pallas_call(
        functools.partial(_gdn_prompt_kernel, tt=tt, c=c),
        grid=(b, t // tt),
        in_specs=[tile(w3), tile(GDN_WIDTH), tile(LANES), _const_spec(conv_w.shape),
                  _const_spec((1, LANES)), _const_spec((1, LANES)), _const_spec((1, hd))],
        out_specs=[tile(GDN_WIDTH),
                   pl.BlockSpec((1, GDN_HEADS, hd, hd), lambda bi, ti: (bi, 0, 0, 0)),
                   pl.BlockSpec((1, GDN_CONV - 1, w3), lambda bi, ti: (bi, 0, 0))],
        out_shape=[jax.ShapeDtypeStruct((b, t, GDN_WIDTH), F32),
                   jax.ShapeDtypeStruct((b, GDN_HEADS, hd, hd), F32),
                   jax.ShapeDtypeStruct((b, GDN_CONV - 1, w3), F32)],
        scratch_shapes=[pltpu.VMEM((tt + SUBLANES, w3), F32),
                        pltpu.VMEM((GDN_HEADS, hd, hd), F32),
                        pltpu.VMEM((tt, GDN_WIDTH), F32),
                        pltpu.VMEM((tt, GDN_WIDTH), F32),
                        pltpu.VMEM((tt, GDN_WIDTH), F32),
                        pltpu.VMEM((tt, LANES), F32),
                        pltpu.VMEM((tt, LANES), F32),
                        pltpu.VMEM((nch, LANES), F32),
                        pltpu.VMEM((nch * GDN_HEADS, hd, hd), BF16),
                        pltpu.VMEM((nch * GDN_HEADS, hd, hd), F32),
                        pltpu.VMEM((nch * GDN_HEADS, c, hd), BF16),
                        pltpu.VMEM((nch * GDN_HEADS, c, hd), F32)],
        compiler_params=_params(("parallel", "arbitrary")),
        name="gdn_prompt",
    )(gqkv, z, ab, conv_w, alog_pad, dtb_pad, norm_w.reshape(1, hd))


def _gdn_sample_kernel(x_ref, z_ref, ab_ref, cb_ref, s_in_ref, cw_ref, alog_ref, dtb_ref, nw_ref,
                       o_ref, s_out_ref, cb_out_ref):
    hd = GDN_HEAD_DIM
    gw = GDN_WIDTH
    nc = GDN_CONV - 1
    x = x_ref[0]
    cb = cb_ref[0]
    conv = x * cw_ref[nc:nc + 1, :]
    for j in range(nc):
        conv = conv + cb[j:j + 1, :] * cw_ref[j:j + 1, :]
    cb_out_ref[0, 0:nc - 1, :] = cb[1:nc, :]
    cb_out_ref[0, nc - 1:nc, :] = x
    act = _silu(conv)
    abv = ab_ref[0]
    g_all = -jnp.exp(alog_ref[...]) * _softplus(abv + dtb_ref[...])
    e_g = jnp.exp(g_all)
    beta = _sigmoid(abv)
    z = z_ref[0]
    nw = nw_ref[...]
    eye = (lax.broadcasted_iota(jnp.int32, (hd, hd), 0) == lax.broadcasted_iota(jnp.int32, (hd, hd), 1))

    def as_column(row):
        return jnp.sum(jnp.where(eye, row, 0.0), axis=1, keepdims=True)

    for h in range(GDN_HEADS):
        hs = slice(h * hd, (h + 1) * hd)
        qh = act[:, hs]
        kh = act[:, gw + h * hd:gw + (h + 1) * hd]
        vv = act[:, 2 * gw + h * hd:2 * gw + (h + 1) * hd]
        qn = qh * lax.rsqrt(jnp.sum(qh * qh, axis=-1, keepdims=True) + L2_EPS) * (hd ** -0.5)
        kn = kh * lax.rsqrt(jnp.sum(kh * kh, axis=-1, keepdims=True) + L2_EPS)
        eg = e_g[:, h:h + 1]
        bh = beta[:, GDN_HEADS + h:GDN_HEADS + h + 1]
        s = s_in_ref[0, h]
        k_col = as_column(kn)
        v_new = vv * bh - jnp.sum((k_col * (bh * eg)) * s, axis=0, keepdims=True)
        qk = jnp.sum(qn * kn, axis=-1, keepdims=True)
        o = jnp.sum((as_column(qn) * eg) * s, axis=0, keepdims=True) + qk * v_new
        s_out_ref[0, h] = s * eg + k_col * v_new
        o = o * lax.rsqrt(jnp.mean(o * o, axis=-1, keepdims=True) + RMS_EPS) * nw
        o_ref[0, :, hs] = o * _silu(z[:, hs])


def _gdn_sample(gqkv, z, ab, conv_buf, s0, conv_w, alog_pad, dtb_pad, norm_w):
    db, w3 = gqkv.shape
    hd = GDN_HEAD_DIM
    nc = GDN_CONV - 1
    row = lambda wd: pl.BlockSpec((1, 1, wd), lambda bi: (bi, 0, 0))
    st = pl.BlockSpec((1, GDN_HEADS, hd, hd), lambda bi: (bi, 0, 0, 0))
    cbs = pl.BlockSpec((1, nc, w3), lambda bi: (bi, 0, 0))
    o, s_new, cb_new = pl.pallas_call(
        _gdn_sample_kernel,
        grid=(db,),
        in_specs=[row(w3), row(GDN_WIDTH), row(LANES), cbs, st, _const_spec(conv_w.shape),
                  _const_spec((1, LANES)), _const_spec((1, LANES)), _const_spec((1, hd))],
        out_specs=[row(GDN_WIDTH), st, cbs],
        out_shape=[jax.ShapeDtypeStruct((db, 1, GDN_WIDTH), F32),
                   jax.ShapeDtypeStruct((db, GDN_HEADS, hd, hd), F32),
                   jax.ShapeDtypeStruct((db, nc, w3), F32)],
        compiler_params=_params(("parallel",)),
        name="gdn_sample",
    )(gqkv.reshape(db, 1, w3), z.reshape(db, 1, GDN_WIDTH), ab.reshape(db, 1, LANES), conv_buf, s0,
      conv_w, alog_pad, dtb_pad, norm_w.reshape(1, hd))
    return o.reshape(db, GDN_WIDTH), s_new, cb_new


DMA_RING = 8


def _moba_route_kernel(pt_ref, q_ref, pool_ref, o_ref, kbuf, sem, ksum_ref, *, n_pages, ppb):
    b = pl.program_id(0)
    n_seq = pl.num_programs(0)
    n_groups = n_pages // DMA_RING
    bpg = DMA_RING // ppb
    nblk = n_pages // ppb

    def page_copy(seq, g, u, bank):
        page = pt_ref[seq * n_pages + g * DMA_RING + u]
        return pltpu.make_async_copy(pool_ref.at[page], kbuf.at[bank, u], sem.at[bank, u])

    def start_group(seq, g, bank):
        for u in range(DMA_RING):
            page_copy(seq, g, u, bank).start()

    @pl.when(b == 0)
    def _():
        start_group(b, 0, 0)

    def group(g, carry):
        bank = g & 1
        for u in range(DMA_RING):
            page_copy(b, g, u, bank).wait()

        @pl.when(g + 1 < n_groups)
        def _():
            start_group(b, g + 1, 1 - bank)

        @pl.when(jnp.logical_and(g + 1 == n_groups, b + 1 < n_seq))
        def _():
            start_group(b + 1, 0, 1 - bank)

        for kb in range(bpg):
            x = kbuf[bank, kb * ppb]
            for u in range(1, ppb):
                x = x + kbuf[bank, kb * ppb + u]
            n = x.shape[0]
            while n > 1:
                n //= 2
                x = x[:n] + x[n:]
            ksum_ref[g * bpg + kb] = x[0]
        return carry

    lax.fori_loop(0, n_groups, group, 0)

    kmean = ksum_ref[...] * (1.0 / MOBA_BLOCK)
    g = jnp.sum(kmean * q_ref[0][None], axis=-1, keepdims=True)
    jf = lax.broadcasted_iota(jnp.int32, g.shape, 0).astype(F32)
    lane = lax.broadcasted_iota(jnp.int32, (MOBA_HEADS, LANES), 1)
    out = jnp.zeros((MOBA_HEADS, LANES), F32)
    for r in range(MOBA_TOPK):
        m = jnp.max(g, axis=0, keepdims=True)
        idx = jnp.min(jnp.where(g == m, jf, float(nblk)), axis=0, keepdims=True)
        g = jnp.where(jf == idx, -jnp.inf, g)
        out = jnp.where(lane == r, idx[0], out)
    o_ref[0] = out.astype(jnp.int32)


def _moba_route(q3, pool_k, pt_flat, n_pages, ppb):
    db = q3.shape[0]
    _, ps, hh, d = pool_k.shape
    assert n_pages % (2 * DMA_RING) == 0 and DMA_RING % ppb == 0 and ps & (ps - 1) == 0
    return pl.pallas_call(
        functools.partial(_moba_route_kernel, n_pages=n_pages, ppb=ppb),
        grid_spec=pltpu.PrefetchScalarGridSpec(
            num_scalar_prefetch=1,
            grid=(db,),
            in_specs=[pl.BlockSpec((1, hh, d), lambda b, pt: (b, 0, 0)), pl.BlockSpec(memory_space=pl.ANY)],
            out_specs=pl.BlockSpec((1, hh, LANES), lambda b, pt: (b, 0, 0)),
            scratch_shapes=[pltpu.VMEM((2, DMA_RING, ps, hh, d), F32), pltpu.SemaphoreType.DMA((2, DMA_RING)),
                            pltpu.VMEM((n_pages // ppb, hh, d), F32)],
        ),
        out_shape=jax.ShapeDtypeStruct((db, hh, LANES), jnp.int32),
        compiler_params=_params(("arbitrary",)),
        name="moba_sample_route",
    )(pt_flat, q3, pool_k)


def _moba_sample_attn_kernel(pg_ref, bk_ref, slope_ref, q_ref, kn_ref, vn_ref, pk_ref, pv_ref, o_ref,
                             kbuf, vbuf, ksem, vsem, *, n_pair, ppb, ps, past):
    b = pl.program_id(0)
    n_seq = pl.num_programs(0)
    n_groups = n_pair // DMA_RING
    hh, d = MOBA_HEADS, MOBA_HEAD_DIM
    per_head = n_pair // hh
    rows = ps * hh

    def copies(seq, g, u, bank):
        pg = pg_ref[seq * n_pair + g * DMA_RING + u]
        return (pltpu.make_async_copy(pk_ref.at[pg], kbuf.at[bank, u], ksem.at[bank, u]),
                pltpu.make_async_copy(pv_ref.at[pg], vbuf.at[bank, u], vsem.at[bank, u]))

    def start_group(seq, g, bank):
        for u in range(DMA_RING):
            for c in copies(seq, g, u, bank):
                c.start()

    @pl.when(b == 0)
    def _():
        start_group(b, 0, 0)

    q = q_ref[0] * (d ** -0.5)
    qb = q.astype(BF16)
    m0 = jnp.sum(q * kn_ref[0], axis=-1, keepdims=True)
    l0 = jnp.ones((hh, 1), F32)
    acc0 = vn_ref[0]
    col = lax.broadcasted_iota(jnp.int32, (hh, rows), 1)
    rowi = lax.broadcasted_iota(jnp.int32, (hh, rows), 0)
    own = (col & (hh - 1)) == rowi
    col_pos = lax.shift_right_logical(col, int(math.log2(hh))).astype(F32)
    slope = slope_ref[:, 0:1]

    def group(g, carry):
        m, l, acc = carry
        bank = g & 1
        for u in range(DMA_RING):
            for c in copies(b, g, u, bank):
                c.wait()

        @pl.when(g + 1 < n_groups)
        def _():
            start_group(b, g + 1, 1 - bank)

        @pl.when(jnp.logical_and(g + 1 == n_groups, b + 1 < n_seq))
        def _():
            start_group(b + 1, 0, 1 - bank)

        scores = [_dot_nt(qb, kbuf[bank, u].reshape(rows, d).astype(BF16)) for u in range(DMA_RING)]
        mts, ps_ = [], []
        for u in range(DMA_RING):
            t = g * DMA_RING + u
            h = t // per_head
            blk = bk_ref[b * (n_pair // ppb) + t // ppb]
            pos0 = blk * (ppb * ps) + (t % ppb) * ps
            dist = jnp.asarray(past - pos0, F32) - col_pos
            s = jnp.where(jnp.logical_and(own, rowi == h), scores[u] - slope * dist, NEG)
            mts.append(jnp.max(s, axis=1, keepdims=True))
            ps_.append(jnp.exp(s - mts[u]))
        outs = [_dot(ps_[u].astype(BF16), vbuf[bank, u].reshape(rows, d).astype(BF16)) for u in range(DMA_RING)]
        m_new = m
        for mt in mts:
            m_new = jnp.maximum(m_new, mt)
        w = jnp.exp(m - m_new)
        l, acc = w * l, w * acc
        for u in range(DMA_RING):
            w = jnp.exp(mts[u] - m_new)
            l, acc = l + w * jnp.sum(ps_[u], axis=1, keepdims=True), acc + w * outs[u]
        return m_new, l, acc

    m, l, acc = lax.fori_loop(0, n_groups, group, (m0, l0, acc0))
    o_ref[0] = acc / l


def _moba_sample_attn(q3, kn3, vn3, pool_k, pool_v, pages_flat, blocks_flat, slope_rows, n_pair, ppb, past):
    db, hh, d = q3.shape
    _, ps, _, _ = pool_k.shape
    assert n_pair % (2 * DMA_RING) == 0 and n_pair % hh == 0 and hh & (hh - 1) == 0
    row = pl.BlockSpec((1, hh, d), lambda b, pg, bk: (b, 0, 0))
    hbm = pl.BlockSpec(memory_space=pl.ANY)
    return pl.pallas_call(
        functools.partial(_moba_sample_attn_kernel, n_pair=n_pair, ppb=ppb, ps=ps, past=past),
        grid_spec=pltpu.PrefetchScalarGridSpec(
            num_scalar_prefetch=2,
            grid=(db,),
            in_specs=[pl.BlockSpec((hh, LANES), lambda b, pg, bk: (0, 0)), row, row, row, hbm, hbm],
            out_specs=row,
            scratch_shapes=[pltpu.VMEM((2, DMA_RING, ps, hh, d), F32), pltpu.VMEM((2, DMA_RING, ps, hh, d), F32),
                            pltpu.SemaphoreType.DMA((2, DMA_RING)), pltpu.SemaphoreType.DMA((2, DMA_RING))],
        ),
        out_shape=jax.ShapeDtypeStruct((db, hh, d), F32),
        compiler_params=_params(("arbitrary",)),
        name="moba_sample_attn",
    )(pages_flat, blocks_flat, slope_rows, q3, kn3, vn3, pool_k, pool_v)


def _moba_sample(q, k_new, v_new, pool_k, pool_v, page_table, slopes):
    db, n_pages = page_table.shape
    _, ps, hh, d = pool_k.shape
    assert MOBA_BLOCK % ps == 0 and hh == MOBA_HEADS and d == MOBA_HEAD_DIM
    ppb = MOBA_BLOCK // ps
    past = n_pages * ps
    assert past % MOBA_BLOCK == 0 and past // MOBA_BLOCK >= MOBA_TOPK
    q3 = q.reshape(db, hh, d)
    blocks = _moba_route(q3, pool_k, page_table.reshape(-1), n_pages, ppb)[:, :, :MOBA_TOPK]
    pidx = (blocks[..., None] * ppb + jnp.arange(ppb, dtype=jnp.int32)).reshape(db, hh * MOBA_TOPK * ppb)
    pages = jnp.take_along_axis(page_table, pidx, axis=1)
    slope_rows = jnp.broadcast_to(slopes[:, None], (hh, LANES))
    o = _moba_sample_attn(q3, k_new.reshape(db, hh, d), v_new.reshape(db, hh, d), pool_k, pool_v,
                          pages.reshape(-1), blocks.reshape(-1), slope_rows, hh * MOBA_TOPK * ppb, ppb, past)
    return o.reshape(db, hh * d)


def _mid_kernel(om_ref, og_ref, x_ref, wo_ref, nw_ref, wq_ref, h_ref, q_ref):
    half = om_ref.shape[1]
    h = x_ref[...] + (_dot(om_ref[...].astype(BF16), wo_ref[0:half, :])
                      + _dot(og_ref[...].astype(BF16), wo_ref[half:, :]))
    h_ref[...] = h
    q_ref[...] = _dot(_rms(h, nw_ref[...]).astype(BF16), wq_ref[...])


def _mid(o_moba, o_gdn, x, w_out, norm_w, w_xq, tm):
    m, dm = x.shape
    half = o_moba.shape[1]
    assert m % tm == 0
    tile = lambda wd: pl.BlockSpec((tm, wd), lambda i: (i, 0))
    return pl.pallas_call(
        _mid_kernel,
        grid=(m // tm,),
        in_specs=[tile(half), tile(o_gdn.shape[1]), tile(dm), _const_spec(w_out.shape), _const_spec((1, dm)),
                  _const_spec(w_xq.shape)],
        out_specs=[tile(dm), tile(w_xq.shape[1])],
        out_shape=[jax.ShapeDtypeStruct((m, dm), F32), jax.ShapeDtypeStruct((m, w_xq.shape[1]), F32)],
        compiler_params=_params(("parallel",)),
        name="out_proj_xq",
    )(o_moba, o_gdn, x, w_out, norm_w.reshape(1, dm), w_xq)


def _xattn_prompt_kernel(q_ref, mk_ref, mv_ref, o_ref, *, hd):
    for h in range(X_HEADS):
        hs = slice(h * hd, (h + 1) * hd)
        qh = (q_ref[0, :, hs] * (hd ** -0.5)).astype(BF16)
        s = _dot_nt(qh, mk_ref[0, :, hs].astype(BF16))
        m = jnp.max(s, axis=1, keepdims=True)
        p = jnp.exp(s - m)
        l = jnp.sum(p, axis=1, keepdims=True)
        o_ref[0, :, hs] = _dot(p.astype(BF16), mv_ref[0, :, hs].astype(BF16)) / l


def _xattn_prompt(qx, mk, mv, tm):
    b, t, w = qx.shape
    nm = mk.shape[1]
    assert t % tm == 0
    q_spec = pl.BlockSpec((1, tm, w), lambda bi, ti: (bi, ti, 0))
    m_spec = pl.BlockSpec((1, nm, w), lambda bi, ti: (bi, 0, 0))
    return pl.pallas_call(
        functools.partial(_xattn_prompt_kernel, hd=w // X_HEADS),
        grid=(b, t // tm),
        in_specs=[q_spec, m_spec, m_spec],
        out_specs=q_spec,
        out_shape=jax.ShapeDtypeStruct((b, t, w), F32),
        compiler_params=_params(("parallel", "parallel")),
        name="xattn_prompt",
    )(qx, mk, mv)


def _xattn_sample_kernel(q_ref, mk_ref, mv_ref, o_ref, *, hd):
    for h in range(X_HEADS):
        hs = slice(h * hd, (h + 1) * hd)
        qh = q_ref[0, :, hs] * (hd ** -0.5)
        s = jnp.sum(mk_ref[0, :, hs] * qh, axis=-1, keepdims=True)
        m = jnp.max(s, axis=0, keepdims=True)
        p = jnp.exp(s - m)
        l = jnp.sum(p, axis=0, keepdims=True)
        o_ref[0, :, hs] = jnp.sum(p * mv_ref[0, :, hs], axis=0, keepdims=True) / l


def _xattn_sample(qx, mk, mv):
    db, w = qx.shape
    nm = mk.shape[1]
    q_spec = pl.BlockSpec((1, 1, w), lambda bi: (bi, 0, 0))
    m_spec = pl.BlockSpec((1, nm, w), lambda bi: (bi, 0, 0))
    return pl.pallas_call(
        functools.partial(_xattn_sample_kernel, hd=w // X_HEADS),
        grid=(db,),
        in_specs=[q_spec, m_spec, m_spec],
        out_specs=q_spec,
        out_shape=jax.ShapeDtypeStruct((db, 1, w), F32),
        compiler_params=_params(("parallel",)),
        name="xattn_sample",
    )(qx.reshape(db, 1, w), mk, mv).reshape(db, w)


def _tail_kernel(h_ref, ox_ref, wxo_ref, nf_ref, wgu_ref, wd_ref, fn_ref, y_ref, *, d_ff, chunk):
    h = h_ref[...] + _dot(ox_ref[...].astype(BF16), wxo_ref[...])
    hn = _rms(h, nf_ref[...]).astype(BF16)
    acc = jnp.zeros(h.shape, F32)
    for c0 in range(0, d_ff, chunk):
        g = _dot(hn, wgu_ref[:, c0:c0 + chunk])
        u = _dot(hn, wgu_ref[:, d_ff + c0:d_ff + c0 + chunk])
        acc = acc + _dot((_silu(g) * u).astype(BF16), wd_ref[c0:c0 + chunk, :])
    y_ref[...] = _rms(h + acc, fn_ref[...])


def _tail(h, ox, w_xo, norm_ffn_w, w_gu, w_down, final_norm_w, tm):
    m, dm = h.shape
    d_ff = w_down.shape[0]
    chunk = 2 * LANES
    assert m % tm == 0 and d_ff % chunk == 0
    tile = lambda wd: pl.BlockSpec((tm, wd), lambda i: (i, 0))
    return pl.pallas_call(
        functools.partial(_tail_kernel, d_ff=d_ff, chunk=chunk),
        grid=(m // tm,),
        in_specs=[tile(dm), tile(ox.shape[1]), _const_spec(w_xo.shape), _const_spec((1, dm)),
                  _const_spec(w_gu.shape), _const_spec(w_down.shape), _const_spec((1, dm))],
        out_specs=tile(dm),
        out_shape=jax.ShapeDtypeStruct((m, dm), F32),
        compiler_params=_params(("parallel",)),
        name="xo_swiglu_norm",
    )(h, ox, w_xo, norm_ffn_w.reshape(1, dm), w_gu, w_down, final_norm_w.reshape(1, dm))


def _row_tile(m, pref):
    return pref if m % pref == 0 else m


def kernel(x_prompt, x_sample, cache_k, cache_v, page_table, state_conv, state_gdn, cache_mem_k, cache_mem_v, mem_prompt, norm_mix_w, w_in, conv_w, a_log, dt_bias, gdn_norm_w, w_out, norm_x_w, mem_norm_w, w_xq, w_xkv, w_xo, norm_ffn_w, w_gu, w_down, final_norm_w):
    assert w_in.shape[0] == 1, "one layer"
    b, t, dm = x_prompt.shape
    db, ds, _ = x_sample.shape
    assert ds == 1
    mw, gw = MOBA_WIDTH, GDN_WIDTH
    n_in = 3 * mw + 4 * gw + 2 * GDN_HEADS
    assert w_in.shape[2] == n_in

    w_in_p = jnp.pad(w_in[0], ((0, 0), (0, 3 * mw + 4 * gw + LANES - n_in)))
    w_in_hi = w_in_p.astype(BF16)
    w_in_lo = (w_in_p[:, :2 * mw] - w_in_hi[:, :2 * mw].astype(F32)).astype(BF16)
    in_splits = (mw, mw, mw, 3 * gw, gw, LANES)
    w_out_b, w_xq_b, w_xo_b = w_out[0].astype(BF16), w_xq[0].astype(BF16), w_xo[0].astype(BF16)
    w_gu_b, w_down_b, w_xkv_b = w_gu[0].astype(BF16), w_down[0].astype(BF16), w_xkv[0].astype(BF16)
    xw = w_xq.shape[2]
    pad_h = lambda v: jnp.pad(v.astype(F32), (0, LANES - GDN_HEADS)).reshape(1, LANES)
    alog_pad, dtb_pad = pad_h(a_log[0]), pad_h(dt_bias[0])
    slopes = jnp.exp2(-8.0 * jnp.arange(1, MOBA_HEADS + 1, dtype=F32) / MOBA_HEADS)

    mp = b * t
    xp = x_prompt.reshape(mp, dm)
    qm, km, vm, gqkv, z, ab = _norm_matmul(xp, norm_mix_w[0], w_in_hi, w_in_lo, in_splits, _row_tile(mp, 512))
    mk_p, mv_p = _norm_matmul(mem_prompt.reshape(-1, dm), mem_norm_w[0], w_xkv_b, None, (xw, xw),
                              _row_tile(mem_prompt.shape[0] * mem_prompt.shape[1], 512))
    n_mem = mem_prompt.shape[1]
    o_moba = _moba_prompt(qm.reshape(b, t, mw), km.reshape(b, t, mw), vm.reshape(b, t, mw), slopes)
    o_gdn, s_p, cb_p = _gdn_prompt(gqkv.reshape(b, t, 3 * gw), z.reshape(b, t, gw), ab.reshape(b, t, LANES),
                                   conv_w[0], alog_pad, dtb_pad, gdn_norm_w[0], _row_tile(t, 512))
    h1, qx = _mid(o_moba.reshape(mp, mw), o_gdn.reshape(mp, gw), xp, w_out_b, norm_x_w[0], w_xq_b, _row_tile(mp, 512))
    ox = _xattn_prompt(qx.reshape(b, t, xw), mk_p.reshape(b, n_mem, xw), mv_p.reshape(b, n_mem, xw), _row_tile(t, 512))
    y_p = _tail(h1, ox.reshape(mp, xw), w_xo_b, norm_ffn_w[0], w_gu_b, w_down_b, final_norm_w, _row_tile(mp, 256))

    xs = x_sample.reshape(db, dm)
    qs, ks, vs, gqkv_s, z_s, ab_s = _norm_matmul(xs, norm_mix_w[0], w_in_hi, w_in_lo, in_splits, db)
    o_moba_s = _moba_sample(qs, ks, vs, cache_k[0], cache_v[0], page_table, slopes)
    o_gdn_s, s_s, cb_s = _gdn_sample(gqkv_s, z_s, ab_s, state_conv[0], state_gdn[0], conv_w[0], alog_pad, dtb_pad,
                                     gdn_norm_w[0])
    h1_s, qx_s = _mid(o_moba_s, o_gdn_s, xs, w_out_b, norm_x_w[0], w_xq_b, db)
    ox_s = _xattn_sample(qx_s, cache_mem_k[0].reshape(db, -1, xw), cache_mem_v[0].reshape(db, -1, xw))
    y_s = _tail(h1_s, ox_s, w_xo_b, norm_ffn_w[0], w_gu_b, w_down_b, final_norm_w, db)

    hh, hd = MOBA_HEADS, MOBA_HEAD_DIM
    xh = X_HEADS
    return (y_p.reshape(b, t, dm), y_s.reshape(db, 1, dm),
            km.reshape(1, b, t, hh, hd), vm.reshape(1, b, t, hh, hd),
            cb_p[None], s_p[None],
            mk_p.reshape(1, b, n_mem, xh, xw // xh), mv_p.reshape(1, b, n_mem, xh, xw // xh),
            ks.reshape(1, db, 1, hh, hd), vs.reshape(1, db, 1, hh, hd),
            cb_s[None], s_s[None])
```

```python
import functools
import math

import jax
import jax.numpy as jnp
from jax import lax
from jax.experimental import pallas as pl
from jax.experimental.pallas import tpu as pltpu

F32 = jnp.float32
BF16 = jnp.bfloat16

MOBA_HEADS = 8
MOBA_HEAD_DIM = 64
MOBA_WIDTH = MOBA_HEADS * MOBA_HEAD_DIM
MOBA_BLOCK = 256
MOBA_TOPK = 3
GDN_HEAD_DIM = 128
GDN_HEADS = 4
GDN_WIDTH = GDN_HEADS * GDN_HEAD_DIM
GDN_CONV = 4
GDN_CHUNK = 64
GDN_GROUP = 4
X_HEADS = 4
RMS_EPS = 1e-6
L2_EPS = 1e-6

LANES = 128
SUBLANES = 8
VMEM_LIMIT = 56 * 1024 * 1024
NEG = -1e30


def _params(sem):
    return pltpu.CompilerParams(dimension_semantics=sem, vmem_limit_bytes=VMEM_LIMIT)


def _const_spec(shape):
    nd = len(shape)
    return pl.BlockSpec(shape, lambda *_: (0,) * nd, pipeline_mode=pl.Buffered(1))


def _dot(a, b):
    return jnp.dot(a, b, preferred_element_type=F32)


def _dot_nt(a, b):
    return lax.dot_general(a, b, (((1,), (1,)), ((), ())), preferred_element_type=F32)


def _dot_tn(a, b):
    return lax.dot_general(a, b, (((0,), (0,)), ((), ())), preferred_element_type=F32)


def _split2(x):
    hi = x.astype(BF16)
    lo = (x - hi.astype(F32)).astype(BF16)
    return hi, lo


def _split3(x):
    hi = x.astype(BF16)
    r = x - hi.astype(F32)
    mid = r.astype(BF16)
    lo = (r - mid.astype(F32)).astype(BF16)
    return hi, mid, lo


def _dot3(a, b, dot=_dot):
    ah, al = _split2(a)
    bh, bl = _split2(b)
    return dot(ah, bh) + (dot(ah, bl) + dot(al, bh))


def _dot_exact_lhs(a_bf16, b, dot=_dot):
    bh, bm, bl = _split3(b)
    return dot(a_bf16, bh) + (dot(a_bf16, bm) + dot(a_bf16, bl))


def _rms(x, w):
    return x * lax.rsqrt(jnp.mean(x * x, axis=-1, keepdims=True) + RMS_EPS) * w


def _sigmoid(x):
    return 1.0 / (1.0 + jnp.exp(-x))


def _silu(x):
    return x * _sigmoid(x)


def _softplus(x):
    return jnp.maximum(x, 0.0) + jnp.log1p(jnp.exp(-jnp.abs(x)))


def _norm_matmul_kernel(*refs, splits, n_hi, chunk):
    x_ref, nw_ref, w_ref = refs[:3]
    if n_hi:
        wlo_ref, out_refs = refs[3], refs[4:]
    else:
        out_refs = refs[3:]
    xn = _rms(x_ref[...], nw_ref[...])
    xh = xn.astype(BF16)
    if n_hi:
        xl = (xn - xh.astype(F32)).astype(BF16)
    c0 = 0
    for o_ref, width in zip(out_refs, splits):
        for s in range(0, width, chunk):
            wd = min(chunk, width - s)
            a, b = c0 + s, c0 + s + wd
            acc = _dot(xh, w_ref[:, a:b])
            if b <= n_hi:
                acc = acc + (_dot(xl, w_ref[:, a:b]) + _dot(xh, wlo_ref[:, a:b]))
            o_ref[:, s:s + wd] = acc
        c0 += width


def _norm_matmul(x, norm_w, w_hi, w_lo, splits, tm):
    m, k = x.shape
    n_hi = 0 if w_lo is None else w_lo.shape[1]
    assert m % tm == 0 and sum(splits) == w_hi.shape[1]
    in_specs = [pl.BlockSpec((tm, k), lambda i: (i, 0)), _const_spec((1, k)), _const_spec(w_hi.shape)]
    args = [x, norm_w.reshape(1, k), w_hi]
    if n_hi:
        in_specs.append(_const_spec(w_lo.shape))
        args.append(w_lo)
    return pl.pallas_call(
        functools.partial(_norm_matmul_kernel, splits=tuple(splits), n_hi=n_hi, chunk=512),
        grid=(m // tm,),
        in_specs=in_specs,
        out_specs=[pl.BlockSpec((tm, wd), lambda i: (i, 0)) for wd in splits],
        out_shape=[jax.ShapeDtypeStruct((m, wd), F32) for wd in splits],
        compiler_params=_params(("parallel",)),
        name="norm_matmul",
    )(*args)


def _topk_select(gate, valid, colf, ncol, axis):
    g = jnp.where(valid, gate, -jnp.inf)
    sel = None
    for _ in range(MOBA_TOPK):
        m = jnp.max(g, axis=axis, keepdims=True)
        idx = jnp.min(jnp.where(g == m, colf, float(ncol)), axis=axis, keepdims=True)
        pick = colf == idx
        sel = pick if sel is None else jnp.logical_or(sel, pick)
        g = jnp.where(pick, -jnp.inf, g)
    return jnp.logical_and(sel, valid)


AUG_ALIBI = 9
AUG_SEL0 = 16
V_ROWS = LANES + 16
LOG2E = 1.4426950408889634
MOBA_UNROLL = 4


def _moba_prompt_kernel(coef_ref, q_ref, k_ref, v_ref, o_ref, kaug_ref, vt_ref, kmp_ref, qat_ref, acc_ref,
                        st_ref, p_ref, al_ref, *, blk, nb):
    hp = pl.program_id(1)
    i = pl.program_id(2)
    d = MOBA_HEAD_DIM
    i_f = jnp.asarray(i, F32)

    @pl.when(i == 0)
    def _():
        kmp_ref[...] = jnp.zeros_like(kmp_ref)
        lane = lax.broadcasted_iota(jnp.int32, (blk, LANES), 1)
        rowf = lax.broadcasted_iota(jnp.int32, (blk, LANES), 0).astype(F32)

        def prep(j, carry):
            start = pl.multiple_of(j * blk, blk)
            kj = k_ref[0, pl.ds(start, blk), :]
            kmp_ref[pl.ds(AUG_SEL0 + j, 1), :] = jnp.mean(kj, axis=0, keepdims=True)
            j_f = jnp.asarray(j, F32)
            aug = jnp.where(lane < 3, j_f, jnp.where(lane < 6, 1.0, jnp.where(lane < AUG_ALIBI, rowf, 0.0)))
            aug = jnp.where(lane == AUG_SEL0 + j, 1.0, aug)
            kaug_ref[j, :, 0:LANES] = kj.astype(BF16)
            kaug_ref[j, :, LANES:2 * LANES] = aug.astype(BF16)
            vt_ref[j, 0:LANES, :] = v_ref[0, pl.ds(start, blk), :].T.astype(BF16)
            vt_ref[j, LANES:V_ROWS, :] = jnp.ones((V_ROWS - LANES, blk), BF16)
            return carry

        lax.fori_loop(0, nb, prep, 0)

    qt = q_ref[0].T
    row = lax.broadcasted_iota(jnp.int32, (LANES, blk), 0)
    rowf = row.astype(F32)
    valid = (row >= AUG_SEL0) & (row < AUG_SEL0 + i)
    in_sel = (row >= AUG_SEL0) & (row < AUG_SEL0 + nb)
    kmp = kmp_ref[...]
    for h2 in range(2):
        head = 2 * hp + h2
        qh = jnp.where((row >= d * h2) & (row < d * (h2 + 1)), qt, 0.0)
        gate = _dot3(kmp, qh)
        sel = _topk_select(gate, valid, rowf, LANES, 0)
        aug = jnp.where(in_sel, jnp.where(jnp.logical_or(sel, row == AUG_SEL0 + i), 0.0, NEG), 0.0)
        c_blk = jnp.full((1, blk), -(coef_ref[head, 6] * i_f), F32)
        for li, part in zip((3, 4, 5), _split3(c_blk)):
            aug = jnp.where(row == li, part.astype(F32), aug)
        for li, ci in ((0, 0), (1, 1), (2, 2), (6, 3), (7, 4), (8, 5)):
            aug = jnp.where(row == li, coef_ref[head, ci], aug)
        qat_ref[h2, 0:LANES, :] = (qh * (d ** -0.5 * LOG2E)).astype(BF16)
        qat_ref[h2, LANES:2 * LANES, :] = aug.astype(BF16)

    def stage_a(slot, kb):
        for h2 in range(2):
            st_ref[slot, h2] = _dot(kaug_ref[kb], qat_ref[h2])

    def stage_b(slot, ms):
        new = []
        for h2 in range(2):
            st = st_ref[slot, h2]
            m_new = jnp.maximum(ms[h2], jnp.max(st, axis=0, keepdims=True))
            al_ref[slot, h2] = jnp.exp2(ms[h2] - m_new)
            p_ref[slot, h2] = jnp.exp2(st - m_new).astype(BF16)
            new.append(m_new)
        return tuple(new)

    def stage_c(slot, kb):
        for h2 in range(2):
            acc_ref[h2] = al_ref[slot, h2] * acc_ref[h2] + _dot(vt_ref[kb], p_ref[slot, h2])

    acc_ref[...] = jnp.zeros_like(acc_ref)
    p_ref[1] = jnp.zeros(p_ref.shape[1:], BF16)
    al_ref[1] = jnp.ones(al_ref.shape[1:], F32)
    keyi = lax.broadcasted_iota(jnp.int32, (blk, blk), 0)
    qryi = lax.broadcasted_iota(jnp.int32, (blk, blk), 1)
    for h2 in range(2):
        st_ref[0, h2] = jnp.where(keyi <= qryi, _dot(kaug_ref[i], qat_ref[h2]), NEG)
    ms = (jnp.full((1, blk), NEG, F32),) * 2

    def step(s, slot, ms):
        stage_c(slot, jnp.where(s == 2, i, jnp.maximum(s - 3, 0)))
        ms = stage_b(1 - slot, ms)
        stage_a(slot, s - 1)
        return ms

    def unrolled(t, ms):
        for u in range(MOBA_UNROLL):
            ms = step(MOBA_UNROLL * t + 1 + u, (1 + u) % 2, ms)
        return ms

    n_full = i // MOBA_UNROLL
    ms = lax.fori_loop(0, n_full, unrolled, ms)
    ms = lax.fori_loop(MOBA_UNROLL * n_full + 1, i + 1, lambda s, ms: step(s, s & 1, ms), ms)
    last = i & 1
    stage_c(1 - last, jnp.where(i == 1, i, jnp.maximum(i - 2, 0)))
    stage_b(last, ms)
    stage_c(last, jnp.maximum(i - 1, 0))

    outs = []
    for h2 in range(2):
        acc = acc_ref[h2]
        outs.append(acc[0:LANES, :] / acc[LANES:LANES + 1, :])
    o_ref[0] = jnp.where(row < d, outs[0], outs[1]).T


def _moba_prompt(q, k, v, slopes):
    b, t, w = q.shape
    blk = MOBA_BLOCK
    assert t % blk == 0 and w == MOBA_WIDTH and blk == 2 * LANES
    nb = t // blk
    assert AUG_SEL0 + nb <= LANES and nb <= 256 and AUG_ALIBI <= AUG_SEL0
    c1 = slopes * LOG2E
    c256 = c1 * blk
    coef = jnp.stack([p.astype(F32) for p in _split3(c256)] + [p.astype(F32) for p in _split3(c1)]
                     + [c256, jnp.zeros_like(c1)], axis=1)
    kv_spec = pl.BlockSpec((1, t, LANES), lambda bi, hp, i: (bi, 0, hp))
    q_spec = pl.BlockSpec((1, blk, LANES), lambda bi, hp, i: (bi, i, hp))
    return pl.pallas_call(
        functools.partial(_moba_prompt_kernel, blk=blk, nb=nb),
        grid=(b, w // LANES, nb),
        in_specs=[pl.BlockSpec(memory_space=pltpu.SMEM), q_spec, kv_spec, kv_spec],
        out_specs=q_spec,
        out_shape=jax.ShapeDtypeStruct((b, t, w), F32),
        scratch_shapes=[pltpu.VMEM((nb, blk, 2 * LANES), BF16),
                        pltpu.VMEM((nb, V_ROWS, blk), BF16),
                        pltpu.VMEM((LANES, LANES), F32),
                        pltpu.VMEM((2, 2 * LANES, blk), BF16),
                        pltpu.VMEM((2, V_ROWS, blk), F32),
                        pltpu.VMEM((2, 2, blk, blk), F32),
                        pltpu.VMEM((2, 2, blk, blk), BF16),
                        pltpu.VMEM((2, 2, 1, blk), F32)],
        compiler_params=_params(("parallel", "parallel", "arbitrary")),
        name="moba_prompt",
    )(coef, q, k, v)


def _tri_inverses(lows, c):
    row = lax.broadcasted_iota(jnp.int32, (c, c), 0)
    col = lax.broadcasted_iota(jnp.int32, (c, c), 1)
    eye = jnp.where(row == col, 1.0, 0.0)
    ts = [eye - low for low in lows]
    ps = [low.astype(BF16) for low in lows]
    for _ in range(int(math.log2(c)) - 1):
        ps = [_dot(p, p).astype(BF16) for p in ps]
        ts = [t + _dot(t.astype(BF16), p) for t, p in zip(ts, ps)]
    return ts


def _gdn_prompt_kernel(x_ref, z_ref, ab_ref, cw_ref, alog_ref, dtb_ref, nw_ref,
                       o_ref, s_out_ref, cb_out_ref,
                       xext_ref, s_ref, qn_ref, kn_ref, vv_ref, gc_ref, beta_ref,
                       el_ref, m_ref, n_ref, qe_ref, o0_ref, *, tt, c):
    t_idx = pl.program_id(1)
    nt = pl.num_programs(1)
    hd = GDN_HEAD_DIM
    gw = GDN_WIDTH
    halo = SUBLANES

    @pl.when(t_idx == 0)
    def _():
        s_ref[...] = jnp.zeros_like(s_ref)
        xext_ref[0:halo, :] = jnp.zeros((halo, 3 * gw), F32)

    xext_ref[halo:halo + tt, :] = x_ref[0]
    conv = xext_ref[halo:halo + tt, :] * cw_ref[GDN_CONV - 1:GDN_CONV, :]
    for j in range(GDN_CONV - 1):
        off = halo - (GDN_CONV - 1) + j
        conv = conv + xext_ref[off:off + tt, :] * cw_ref[j:j + 1, :]

    @pl.when(t_idx == nt - 1)
    def _():
        cb_out_ref[0] = xext_ref[halo + tt - (GDN_CONV - 1):halo + tt, :]

    xext_ref[0:halo, :] = xext_ref[tt:tt + halo, :]

    act = _silu(conv)
    for h in range(GDN_HEADS):
        qh = act[:, h * hd:(h + 1) * hd]
        kh = act[:, gw + h * hd:gw + (h + 1) * hd]
        qn_ref[:, h * hd:(h + 1) * hd] = qh * lax.rsqrt(jnp.sum(qh * qh, axis=-1, keepdims=True) + L2_EPS) * (hd ** -0.5)
        kn_ref[:, h * hd:(h + 1) * hd] = kh * lax.rsqrt(jnp.sum(kh * kh, axis=-1, keepdims=True) + L2_EPS)
    vv_ref[...] = act[:, 2 * gw:]

    abv = ab_ref[0]
    g_all = -jnp.exp(alog_ref[...]) * _softplus(abv + dtb_ref[...])
    beta_ref[...] = _sigmoid(abv)
    rc = lax.broadcasted_iota(jnp.int32, (c, c), 0)
    cc = lax.broadcasted_iota(jnp.int32, (c, c), 1)
    tri = jnp.where(rc >= cc, 1.0, 0.0).astype(BF16)
    for ci in range(tt // c):
        gc_ref[ci * c:(ci + 1) * c, :] = _dot_exact_lhs(tri, g_all[ci * c:(ci + 1) * c, :])

    incl = rc >= cc
    strict = rc > cc
    lane8 = lax.broadcasted_iota(jnp.int32, (SUBLANES, LANES), 1)
    nw = nw_ref[...]

    def prepare_group(gi, carry):
        probs = []
        for cj in range(GDN_GROUP):
            ci = gi * GDN_GROUP + cj
            r0 = pl.multiple_of(ci * c, c)
            gcs = gc_ref[pl.ds(r0, c), :]
            bet = beta_ref[pl.ds(r0, c), :]
            e_gc = jnp.exp(gcs)
            g_last = gcs[c - 1:c, :]
            e_rev = jnp.exp(g_last - gcs)
            el_ref[pl.ds(ci, 1), :] = jnp.exp(g_last)
            for h in range(GDN_HEADS):
                hs = slice(h * hd, (h + 1) * hd)
                onehot = jnp.where(lane8 == h, 1.0, 0.0).astype(BF16)
                grow = _dot_exact_lhs(onehot, gcs, _dot_nt)[0:1, :]
                probs.append(dict(
                    idx=ci * GDN_HEADS + h, qn=qn_ref[pl.ds(r0, c), hs], kn=kn_ref[pl.ds(r0, c), hs],
                    vv=vv_ref[pl.ds(r0, c), hs], bcol=bet[:, GDN_HEADS + h:GDN_HEADS + h + 1],
                    egc=e_gc[:, h:h + 1], erev=e_rev[:, h:h + 1],
                    decay=jnp.where(incl, jnp.exp(gcs[:, h:h + 1] - grow), 0.0)))
        for p in probs:
            p["knb"] = p["kn"].astype(BF16)
        lows = [jnp.where(strict, p["bcol"] * _dot_nt(p["knb"], p["knb"]) * p["decay"], 0.0) for p in probs]
        tinvs = _tri_inverses(lows, c)
        sols = []
        for p, tinv in zip(probs, tinvs):
            rhs = jnp.concatenate([p["kn"] * (p["bcol"] * p["egc"]), p["vv"] * p["bcol"]], axis=1)
            sols.append(_dot(tinv.astype(BF16), rhs.astype(BF16)).astype(BF16))
        for p, sol in zip(probs, sols):
            aqk = (_dot_nt(p["qn"].astype(BF16), p["knb"]) * p["decay"]).astype(BF16)
            upd = _dot_tn((p["kn"] * p["erev"]).astype(BF16), sol)
            out = _dot(aqk, sol)
            m_ref[p["idx"]] = upd[:, 0:hd].astype(BF16)
            n_ref[p["idx"]] = upd[:, hd:]
            qe_ref[p["idx"]] = (p["qn"] * p["egc"] - out[:, 0:hd]).astype(BF16)
            o0_ref[p["idx"]] = out[:, hd:]
        return carry

    lax.fori_loop(0, tt // (GDN_GROUP * c), prepare_group, 0)

    def advance(ci, carry):
        r0 = pl.multiple_of(ci * c, c)
        e_last = el_ref[pl.ds(ci, 1), :]
        for h in range(GDN_HEADS):
            hs = slice(h * hd, (h + 1) * hd)
            idx = ci * GDN_HEADS + h
            s = s_ref[h]
            sb = s.astype(BF16)
            o = _dot(qe_ref[idx], sb) + o0_ref[idx]
            s_ref[h] = s * e_last[:, h:h + 1] - _dot(m_ref[idx], sb) + n_ref[idx]
            o = o * lax.rsqrt(jnp.mean(o * o, axis=-1, keepdims=True) + RMS_EPS) * nw
            o_ref[0, pl.ds(r0, c), hs] = o * _silu(z_ref[0, pl.ds(r0, c), hs])
        return carry

    lax.fori_loop(0, tt // c, advance, 0)

    @pl.when(t_idx == nt - 1)
    def _():
        s_out_ref[0] = s_ref[...]


def _gdn_prompt(gqkv, z, ab, conv_w, alog_pad, dtb_pad, norm_w, tt):
    b, t, w3 = gqkv.shape
    c = GDN_CHUNK
    assert t % tt == 0 and tt % (GDN_GROUP * c) == 0 and w3 == 3 * GDN_WIDTH
    hd = GDN_HEAD_DIM
    nch = tt // c
    tile = lambda wd: pl.BlockSpec((1, tt, wd), lambda bi, ti: (bi, ti, 0))
    return pl.pallas_call(
        functools.partial(_gdn_prompt_kernel, tt=tt, c=c),
        grid=(b, t // tt),
        in_specs=[tile(w3), tile(GDN_WIDTH), tile(LANES), _const_spec(conv_w.shape),
                  _const_spec((1, LANES)), _const_spec((1, LANES)), _const_spec((1, hd))],
        out_specs=[tile(GDN_WIDTH),
                   pl.BlockSpec((1, GDN_HEADS, hd, hd), lambda bi, ti: (bi, 0, 0, 0)),
                   pl.BlockSpec((1, GDN_CONV - 1, w3), lambda bi, ti: (bi, 0, 0))],
        out_shape=[jax.ShapeDtypeStruct((b, t, GDN_WIDTH), F32),
                   jax.ShapeDtypeStruct((b, GDN_HEADS, hd, hd), F32),
                   jax.ShapeDtypeStruct((b, GDN_CONV - 1, w3), F32)],
        scratch_shapes=[pltpu.VMEM((tt + SUBLANES, w3), F32),
                        pltpu.VMEM((GDN_HEADS, hd, hd), F32),
                        pltpu.VMEM((tt, GDN_WIDTH), F32),
                        pltpu.VMEM((tt, GDN_WIDTH), F32),
                        pltpu.VMEM((tt, GDN_WIDTH), F32),
                        pltpu.VMEM((tt, LANES), F32),
                        pltpu.VMEM((tt, LANES), F32),
                        pltpu.VMEM((nch, LANES), F32),
                        pltpu.VMEM((nch * GDN_HEADS, hd, hd), BF16),
                        pltpu.VMEM((nch * GDN_HEADS, hd, hd), F32),
                        pltpu.VMEM((nch * GDN_HEADS, c, hd), BF16),
                        pltpu.VMEM((nch * GDN_HEADS, c, hd), F32)],
        compiler_params=_params(("parallel", "arbitrary")),
        name="gdn_prompt",
    )(gqkv, z, ab, conv_w, alog_pad, dtb_pad, norm_w.reshape(1, hd))


def _gdn_sample_kernel(x_ref, z_ref, ab_ref, cb_ref, s_in_ref, cw_ref, alog_ref, dtb_ref, nw_ref,
                       o_ref, s_out_ref, cb_out_ref):
    hd = GDN_HEAD_DIM
    gw = GDN_WIDTH
    nc = GDN_CONV - 1
    x = x_ref[0]
    cb = cb_ref[0]
    conv = x * cw_ref[nc:nc + 1, :]
    for j in range(nc):
        conv = conv + cb[j:j + 1, :] * cw_ref[j:j + 1, :]
    cb_out_ref[0, 0:nc - 1, :] = cb[1:nc, :]
    cb_out_ref[0, nc - 1:nc, :] = x
    act = _silu(conv)
    abv = ab_ref[0]
    g_all = -jnp.exp(alog_ref[...]) * _softplus(abv + dtb_ref[...])
    e_g = jnp.exp(g_all)
    beta = _sigmoid(abv)
    z = z_ref[0]
    nw = nw_ref[...]
    eye = (lax.broadcasted_iota(jnp.int32, (hd, hd), 0) == lax.broadcasted_iota(jnp.int32, (hd, hd), 1))

    def as_column(row):
        return jnp.sum(jnp.where(eye, row, 0.0), axis=1, keepdims=True)

    for h in range(GDN_HEADS):
        hs = slice(h * hd, (h + 1) * hd)
        qh = act[:, hs]
        kh = act[:, gw + h * hd:gw + (h + 1) * hd]
        vv = act[:, 2 * gw + h * hd:2 * gw + (h + 1) * hd]
        qn = qh * lax.rsqrt(jnp.sum(qh * qh, axis=-1, keepdims=True) + L2_EPS) * (hd ** -0.5)
        kn = kh * lax.rsqrt(jnp.sum(kh * kh, axis=-1, keepdims=True) + L2_EPS)
        eg = e_g[:, h:h + 1]
        bh = beta[:, GDN_HEADS + h:GDN_HEADS + h + 1]
        s = s_in_ref[0, h]
        k_col = as_column(kn)
        v_new = vv * bh - jnp.sum((k_col * (bh * eg)) * s, axis=0, keepdims=True)
        qk = jnp.sum(qn * kn, axis=-1, keepdims=True)
        o = jnp.sum((as_column(qn) * eg) * s, axis=0, keepdims=True) + qk * v_new
        s_out_ref[0, h] = s * eg + k_col * v_new
        o = o * lax.rsqrt(jnp.mean(o * o, axis=-1, keepdims=True) + RMS_EPS) * nw
        o_ref[0, :, hs] = o * _silu(z[:, hs])


def _gdn_sample(gqkv, z, ab, conv_buf, s0, conv_w, alog_pad, dtb_pad, norm_w):
    db, w3 = gqkv.shape
    hd = GDN_HEAD_DIM
    nc = GDN_CONV - 1
    row = lambda wd: pl.BlockSpec((1, 1, wd), lambda bi: (bi, 0, 0))
    st = pl.BlockSpec((1, GDN_HEADS, hd, hd), lambda bi: (bi, 0, 0, 0))
    cbs = pl.BlockSpec((1, nc, w3), lambda bi: (bi, 0, 0))
    o, s_new, cb_new = pl.pallas_call(
        _gdn_sample_kernel,
        grid=(db,),
        in_specs=[row(w3), row(GDN_WIDTH), row(LANES), cbs, st, _const_spec(conv_w.shape),
                  _const_spec((1, LANES)), _const_spec((1, LANES)), _const_spec((1, hd))],
        out_specs=[row(GDN_WIDTH), st, cbs],
        out_shape=[jax.ShapeDtypeStruct((db, 1, GDN_WIDTH), F32),
                   jax.ShapeDtypeStruct((db, GDN_HEADS, hd, hd), F32),
                   jax.ShapeDtypeStruct((db, nc, w3), F32)],
        compiler_params=_params(("parallel",)),
        name="gdn_sample",
    )(gqkv.reshape(db, 1, w3), z.reshape(db, 1, GDN_WIDTH), ab.reshape(db, 1, LANES), conv_buf, s0,
      conv_w, alog_pad, dtb_pad, norm_w.reshape(1, hd))
    return o.reshape(db, GDN_WIDTH), s_new, cb_new


DMA_RING = 8


def _moba_route_kernel(pt_ref, q_ref, pool_ref, o_ref, kbuf, sem, qc_ref, *, n_pages, ppb):
    b = pl.program_id(0)
    n_seq = pl.num_programs(0)
    n_groups = n_pages // DMA_RING
    bpg = DMA_RING // ppb
    nblk = n_pages // ppb
    hh, d = MOBA_HEADS, MOBA_HEAD_DIM

    def page_copy(seq, g, u, bank):
        page = pt_ref[seq * n_pages + g * DMA_RING + u]
        return pltpu.make_async_copy(pool_ref.at[page], kbuf.at[bank, u], sem.at[bank, u])

    def start_group(seq, g, bank):
        for u in range(DMA_RING):
            page_copy(seq, g, u, bank).start()

    @pl.when(b == 0)
    def _():
        start_group(b, 0, 0)

    q = q_ref[0]
    eye = lax.broadcasted_iota(jnp.int32, (d, d), 0) == lax.broadcasted_iota(jnp.int32, (d, d), 1)
    for h in range(hh):
        q_col = jnp.sum(jnp.where(eye, q[h:h + 1, :], 0.0), axis=1, keepdims=True)
        qc_ref[h] = jnp.broadcast_to(q_col, (d, LANES))
    lane = lax.broadcasted_iota(jnp.int32, (hh, LANES), 1)
    rowi = lax.broadcasted_iota(jnp.int32, (hh, LANES), 0)

    def group(g, gate):
        bank = g & 1
        for u in range(DMA_RING):
            page_copy(b, g, u, bank).wait()

        @pl.when(g + 1 < n_groups)
        def _():
            start_group(b, g + 1, 1 - bank)

        @pl.when(jnp.logical_and(g + 1 == n_groups, b + 1 < n_seq))
        def _():
            start_group(b + 1, 0, 1 - bank)

        for kb in range(bpg):
            x = kbuf[bank, kb * ppb]
            for u in range(1, ppb):
                x = x + kbuf[bank, kb * ppb + u]
            w = (x * qc_ref[...]).reshape(hh, d // SUBLANES, SUBLANES, LANES)
            r = jnp.sum(w, axis=1)
            per_head = jnp.zeros((hh, LANES), F32)
            for h in range(hh):
                per_head = jnp.where(rowi == h, jnp.sum(r[h], axis=0, keepdims=True), per_head)
            total = jnp.sum(per_head, axis=1, keepdims=True)
            gate = jnp.where(lane == g * bpg + kb, total, gate)
        return gate

    gate = lax.fori_loop(0, n_groups, group, jnp.zeros((hh, LANES), F32)) * (1.0 / MOBA_BLOCK)
    lanef = lane.astype(F32)
    gate = jnp.where(lane < nblk, gate, -jnp.inf)
    out = jnp.zeros((hh, LANES), F32)
    for r in range(MOBA_TOPK):
        m = jnp.max(gate, axis=1, keepdims=True)
        idx = jnp.min(jnp.where(gate == m, lanef, float(LANES)), axis=1, keepdims=True)
        gate = jnp.where(lanef == idx, -jnp.inf, gate)
        out = jnp.where(lane == r, idx, out)
    o_ref[0] = out.astype(jnp.int32)


def _moba_route(q3, pool_kt, pt_flat, n_pages, ppb):
    db = q3.shape[0]
    _, hh, d, ps = pool_kt.shape
    assert n_pages % (2 * DMA_RING) == 0 and DMA_RING % ppb == 0 and ps == LANES and n_pages // ppb <= LANES
    return pl.pallas_call(
        functools.partial(_moba_route_kernel, n_pages=n_pages, ppb=ppb),
        grid_spec=pltpu.PrefetchScalarGridSpec(
            num_scalar_prefetch=1,
            grid=(db,),
            in_specs=[pl.BlockSpec((1, hh, d), lambda b, pt: (b, 0, 0)), pl.BlockSpec(memory_space=pl.ANY)],
            out_specs=pl.BlockSpec((1, hh, LANES), lambda b, pt: (b, 0, 0)),
            scratch_shapes=[pltpu.VMEM((2, DMA_RING, hh, d, ps), F32), pltpu.SemaphoreType.DMA((2, DMA_RING)),
                            pltpu.VMEM((hh, d, LANES), F32)],
        ),
        out_shape=jax.ShapeDtypeStruct((db, hh, LANES), jnp.int32),
        compiler_params=_params(("arbitrary",)),
        name="moba_sample_route",
    )(pt_flat, q3, pool_kt)


def _moba_sample_attn_kernel(pg_ref, bk_ref, slope_ref, q_ref, kn_ref, vn_ref, pk_ref, pv_ref, o_ref,
                             kbuf, vbuf, ksem, vsem, *, per_head, ppb, ps, past):
    b = pl.program_id(0)
    n_seq = pl.num_programs(0)
    hh, d = MOBA_HEADS, MOBA_HEAD_DIM
    n_slab = hh * per_head

    def copies(seq, t, bank):
        pg = pg_ref[seq * n_slab + t]
        h = t // per_head
        return (pltpu.make_async_copy(pk_ref.at[pg, h], kbuf.at[bank, t], ksem.at[bank, t]),
                pltpu.make_async_copy(pv_ref.at[pg, h], vbuf.at[bank, t], vsem.at[bank, t]))

    def start_all(seq, bank):
        for t in range(n_slab):
            for c in copies(seq, t, bank):
                c.start()

    @pl.when(b == 0)
    def _():
        start_all(b, 0)

    bank = b & 1
    for t in range(n_slab):
        for c in copies(b, t, bank):
            c.wait()

    @pl.when(b + 1 < n_seq)
    def _():
        start_all(b + 1, 1 - bank)

    q = q_ref[0] * (d ** -0.5)
    kn = kn_ref[0]
    vn = vn_ref[0]
    lanef = lax.broadcasted_iota(jnp.int32, (1, ps), 1).astype(F32)
    for h in range(hh):
        qh = jnp.broadcast_to(q[h:h + 1, :], (SUBLANES, d)).astype(BF16)
        rows = []
        for j in range(per_head):
            t = h * per_head + j
            blk = bk_ref[b * (n_slab // ppb) + t // ppb]
            pos0 = blk * (ppb * ps) + (t % ppb) * ps
            s = _dot(qh, kbuf[bank, t].astype(BF16))[0:1, :]
            rows.append(s - slope_ref[h] * (jnp.asarray(past - pos0, F32) - lanef))
        s_all = jnp.concatenate(rows, axis=1)
        s_new = jnp.sum(q[h:h + 1, :] * kn[h:h + 1, :], axis=1, keepdims=True)
        m = jnp.maximum(jnp.max(s_all, axis=1, keepdims=True), s_new)
        p = jnp.exp(s_all - m)
        p_new = jnp.exp(s_new - m)
        l = jnp.sum(p, axis=1, keepdims=True) + p_new
        v_all = jnp.concatenate([vbuf[bank, h * per_head + j].astype(BF16) for j in range(per_head)], axis=1)
        o = _dot_nt(jnp.broadcast_to(p, (SUBLANES, p.shape[1])).astype(BF16), v_all)[0:1, :]
        o_ref[0, h:h + 1, :] = (o + p_new * vn[h:h + 1, :]) / l


def _moba_sample_attn(q3, kn3, vn3, pool_kt, pool_vt, pages_flat, blocks_flat, slopes, per_head, ppb, past):
    db, hh, d = q3.shape
    _, _, _, ps = pool_kt.shape
    n_slab = hh * per_head
    row = pl.BlockSpec((1, hh, d), lambda b, pg, bk: (b, 0, 0))
    hbm = pl.BlockSpec(memory_space=pl.ANY)
    return pl.pallas_call(
        functools.partial(_moba_sample_attn_kernel, per_head=per_head, ppb=ppb, ps=ps, past=past),
        grid_spec=pltpu.PrefetchScalarGridSpec(
            num_scalar_prefetch=2,
            grid=(db,),
            in_specs=[pl.BlockSpec(memory_space=pltpu.SMEM), row, row, row, hbm, hbm],
            out_specs=row,
            scratch_shapes=[pltpu.VMEM((2, n_slab, d, ps), F32), pltpu.VMEM((2, n_slab, d, ps), F32),
                            pltpu.SemaphoreType.DMA((2, n_slab)), pltpu.SemaphoreType.DMA((2, n_slab))],
        ),
        out_shape=jax.ShapeDtypeStruct((db, hh, d), F32),
        compiler_params=_params(("arbitrary",)),
        name="moba_sample_attn",
    )(pages_flat, blocks_flat, slopes, q3, kn3, vn3, pool_kt, pool_vt)


def _moba_sample(q, k_new, v_new, pool_k, pool_v, page_table, slopes):
    db, n_pages = page_table.shape
    _, ps, hh, d = pool_k.shape
    assert MOBA_BLOCK % ps == 0 and hh == MOBA_HEADS and d == MOBA_HEAD_DIM
    ppb = MOBA_BLOCK // ps
    past = n_pages * ps
    assert past % MOBA_BLOCK == 0 and past // MOBA_BLOCK >= MOBA_TOPK
    pool_kt = jnp.transpose(pool_k, (0, 2, 3, 1))
    pool_vt = jnp.transpose(pool_v, (0, 2, 3, 1))
    q3 = q.reshape(db, hh, d)
    blocks = _moba_route(q3, pool_kt, page_table.reshape(-1), n_pages, ppb)[:, :, :MOBA_TOPK]
    pidx = (blocks[..., None] * ppb + jnp.arange(ppb, dtype=jnp.int32)).reshape(db, hh * MOBA_TOPK * ppb)
    pages = jnp.take_along_axis(page_table, pidx, axis=1)
    o = _moba_sample_attn(q3, k_new.reshape(db, hh, d), v_new.reshape(db, hh, d), pool_kt, pool_vt,
                          pages.reshape(-1), blocks.reshape(-1), slopes, MOBA_TOPK * ppb, ppb, past)
    return o.reshape(db, hh * d)


def _mid_kernel(om_ref, og_ref, x_ref, wo_ref, nw_ref, wq_ref, h_ref, q_ref):
    half = om_ref.shape[1]
    h = x_ref[...] + (_dot(om_ref[...].astype(BF16), wo_ref[0:half, :])
                      + _dot(og_ref[...].astype(BF16), wo_ref[half:, :]))
    h_ref[...] = h
    q_ref[...] = _dot(_rms(h, nw_ref[...]).astype(BF16), wq_ref[...])


def _mid(o_moba, o_gdn, x, w_out, norm_w, w_xq, tm):
    m, dm = x.shape
    half = o_moba.shape[1]
    assert m % tm == 0
    tile = lambda wd: pl.BlockSpec((tm, wd), lambda i: (i, 0))
    return pl.pallas_call(
        _mid_kernel,
        grid=(m // tm,),
        in_specs=[tile(half), tile(o_gdn.shape[1]), tile(dm), _const_spec(w_out.shape), _const_spec((1, dm)),
                  _const_spec(w_xq.shape)],
        out_specs=[tile(dm), tile(w_xq.shape[1])],
        out_shape=[jax.ShapeDtypeStruct((m, dm), F32), jax.ShapeDtypeStruct((m, w_xq.shape[1]), F32)],
        compiler_params=_params(("parallel",)),
        name="out_proj_xq",
    )(o_moba, o_gdn, x, w_out, norm_w.reshape(1, dm), w_xq)


def _xattn_prompt_kernel(q_ref, mk_ref, mv_ref, o_ref, *, hd):
    for h in range(X_HEADS):
        hs = slice(h * hd, (h + 1) * hd)
        qh = (q_ref[0, :, hs] * (hd ** -0.5)).astype(BF16)
        s = _dot_nt(qh, mk_ref[0, :, hs].astype(BF16))
        m = jnp.max(s, axis=1, keepdims=True)
        p = jnp.exp(s - m)
        l = jnp.sum(p, axis=1, keepdims=True)
        o_ref[0, :, hs] = _dot(p.astype(BF16), mv_ref[0, :, hs].astype(BF16)) / l


def _xattn_prompt(qx, mk, mv, tm):
    b, t, w = qx.shape
    nm = mk.shape[1]
    assert t % tm == 0
    q_spec = pl.BlockSpec((1, tm, w), lambda bi, ti: (bi, ti, 0))
    m_spec = pl.BlockSpec((1, nm, w), lambda bi, ti: (bi, 0, 0))
    return pl.pallas_call(
        functools.partial(_xattn_prompt_kernel, hd=w // X_HEADS),
        grid=(b, t // tm),
        in_specs=[q_spec, m_spec, m_spec],
        out_specs=q_spec,
        out_shape=jax.ShapeDtypeStruct((b, t, w), F32),
        compiler_params=_params(("parallel", "parallel")),
        name="xattn_prompt",
    )(qx, mk, mv)


def _xattn_sample_kernel(q_ref, mk_ref, mv_ref, o_ref, *, hd):
    for h in range(X_HEADS):
        hs = slice(h * hd, (h + 1) * hd)
        qh = q_ref[0, :, hs] * (hd ** -0.5)
        s = jnp.sum(mk_ref[0, :, hs] * qh, axis=-1, keepdims=True)
        m = jnp.max(s, axis=0, keepdims=True)
        p = jnp.exp(s - m)
        l = jnp.sum(p, axis=0, keepdims=True)
        o_ref[0, :, hs] = jnp.sum(p * mv_ref[0, :, hs], axis=0, keepdims=True) / l


def _xattn_sample(qx, mk, mv):
    db, w = qx.shape
    nm = mk.shape[1]
    q_spec = pl.BlockSpec((1, 1, w), lambda bi: (bi, 0, 0))
    m_spec = pl.BlockSpec((1, nm, w), lambda bi: (bi, 0, 0))
    return pl.pallas_call(
        functools.partial(_xattn_sample_kernel, hd=w // X_HEADS),
        grid=(db,),
        in_specs=[q_spec, m_spec, m_spec],
        out_specs=q_spec,
        out_shape=jax.ShapeDtypeStruct((db, 1, w), F32),
        compiler_params=_params(("parallel",)),
        name="xattn_sample",
    )(qx.reshape(db, 1, w), mk, mv).reshape(db, w)


def _tail_kernel(h_ref, ox_ref, wxo_ref, nf_ref, wgu_ref, wd_ref, fn_ref, y_ref, *, d_ff, chunk):
    h = h_ref[...] + _dot(ox_ref[...].astype(BF16), wxo_ref[...])
    hn = _rms(h, nf_ref[...]).astype(BF16)
    acc = jnp.zeros(h.shape, F32)
    for c0 in range(0, d_ff, chunk):
        g = _dot(hn, wgu_ref[:, c0:c0 + chunk])
        u = _dot(hn, wgu_ref[:, d_ff + c0:d_ff + c0 + chunk])
        acc = acc + _dot((_silu(g) * u).astype(BF16), wd_ref[c0:c0 + chunk, :])
    y_ref[...] = _rms(h + acc, fn_ref[...])


def _tail(h, ox, w_xo, norm_ffn_w, w_gu, w_down, final_norm_w, tm):
    m, dm = h.shape
    d_ff = w_down.shape[0]
    chunk = 2 * LANES
    assert m % tm == 0 and d_ff % chunk == 0
    tile = lambda wd: pl.BlockSpec((tm, wd), lambda i: (i, 0))
    return pl.pallas_call(
        functools.partial(_tail_kernel, d_ff=d_ff, chunk=chunk),
        grid=(m // tm,),
        in_specs=[tile(dm), tile(ox.shape[1]), _const_spec(w_xo.shape), _const_spec((1, dm)),
                  _const_spec(w_gu.shape), _const_spec(w_down.shape), _const_spec((1, dm))],
        out_specs=tile(dm),
        out_shape=jax.ShapeDtypeStruct((m, dm), F32),
        compiler_params=_params(("parallel",)),
        name="xo_swiglu_norm",
    )(h, ox, w_xo, norm_ffn_w.reshape(1, dm), w_gu, w_down, final_norm_w.reshape(1, dm))


def _row_tile(m, pref):
    return pref if m % pref == 0 else m


def kernel(x_prompt, x_sample, cache_k, cache_v, page_table, state_conv, state_gdn, cache_mem_k, cache_mem_v, mem_prompt, norm_mix_w, w_in, conv_w, a_log, dt_bias, gdn_norm_w, w_out, norm_x_w, mem_norm_w, w_xq, w_xkv, w_xo, norm_ffn_w, w_gu, w_down, final_norm_w):
    assert w_in.shape[0] == 1, "one layer"
    b, t, dm = x_prompt.shape
    db, ds, _ = x_sample.shape
    assert ds == 1
    mw, gw = MOBA_WIDTH, GDN_WIDTH
    n_in = 3 * mw + 4 * gw + 2 * GDN_HEADS
    assert w_in.shape[2] == n_in

    w_in_p = jnp.pad(w_in[0], ((0, 0), (0, 3 * mw + 4 * gw + LANES - n_in)))
    w_in_hi = w_in_p.astype(BF16)
    w_in_lo = (w_in_p[:, :2 * mw] - w_in_hi[:, :2 * mw].astype(F32)).astype(BF16)
    in_splits = (mw, mw, mw, 3 * gw, gw, LANES)
    w_out_b, w_xq_b, w_xo_b = w_out[0].astype(BF16), w_xq[0].astype(BF16), w_xo[0].astype(BF16)
    w_gu_b, w_down_b, w_xkv_b = w_gu[0].astype(BF16), w_down[0].astype(BF16), w_xkv[0].astype(BF16)
    xw = w_xq.shape[2]
    pad_h = lambda v: jnp.pad(v.astype(F32), (0, LANES - GDN_HEADS)).reshape(1, LANES)
    alog_pad, dtb_pad = pad_h(a_log[0]), pad_h(dt_bias[0])
    slopes = jnp.exp2(-8.0 * jnp.arange(1, MOBA_HEADS + 1, dtype=F32) / MOBA_HEADS)

    mp = b * t
    xp = x_prompt.reshape(mp, dm)
    qm, km, vm, gqkv, z, ab = _norm_matmul(xp, norm_mix_w[0], w_in_hi, w_in_lo, in_splits, _row_tile(mp, 512))
    mk_p, mv_p = _norm_matmul(mem_prompt.reshape(-1, dm), mem_norm_w[0], w_xkv_b, None, (xw, xw),
                              _row_tile(mem_prompt.shape[0] * mem_prompt.shape[1], 512))
    n_mem = mem_prompt.shape[1]
    o_moba = _moba_prompt(qm.reshape(b, t, mw), km.reshape(b, t, mw), vm.reshape(b, t, mw), slopes)
    o_gdn, s_p, cb_p = _gdn_prompt(gqkv.reshape(b, t, 3 * gw), z.reshape(b, t, gw), ab.reshape(b, t, LANES),
                                   conv_w[0], alog_pad, dtb_pad, gdn_norm_w[0], _row_tile(t, 512))
    h1, qx = _mid(o_moba.reshape(mp, mw), o_gdn.reshape(mp, gw), xp, w_out_b, norm_x_w[0], w_xq_b, _row_tile(mp, 512))
    ox = _xattn_prompt(qx.reshape(b, t, xw), mk_p.reshape(b, n_mem, xw), mv_p.reshape(b, n_mem, xw), _row_tile(t, 512))
    y_p = _tail(h1, ox.reshape(mp, xw), w_xo_b, norm_ffn_w[0], w_gu_b, w_down_b, final_norm_w, _row_tile(mp, 256))

    xs = x_sample.reshape(db, dm)
    qs, ks, vs, gqkv_s, z_s, ab_s = _norm_matmul(xs, norm_mix_w[0], w_in_hi, w_in_lo, in_splits, db)
    o_moba_s = _moba_sample(qs, ks, vs, cache_k[0], cache_v[0], page_table, slopes)
    o_gdn_s, s_s, cb_s = _gdn_sample(gqkv_s, z_s, ab_s, state_conv[0], state_gdn[0], conv_w[0], alog_pad, dtb_pad,
                                     gdn_norm_w[0])
    h1_s, qx_s = _mid(o_moba_s, o_gdn_s, xs, w_out_b, norm_x_w[0], w_xq_b, db)
    ox_s = _xattn_sample(qx_s, cache_mem_k[0].reshape(db, -1, xw), cache_mem_v[0].reshape(db, -1, xw))
    y_s = _tail(h1_s, ox_s, w_xo_b, norm_ffn_w[0], w_gu_b, w_down_b, final_norm_w, db)

    hh, hd = MOBA_HEADS, MOBA_HEAD_DIM
    xh = X_HEADS
    return (y_p.reshape(b, t, dm), y_s.reshape(db, 1, dm),
            km.reshape(1, b, t, hh, hd), vm.reshape(1, b, t, hh, hd),
            cb_p[None], s_p[None],
            mk_p.reshape(1, b, n_mem, xh, xw // xh), mv_p.reshape(1, b, n_mem, xh, xw // xh),
            ks.reshape(1, db, 1, hh, hd), vs.reshape(1, db, 1, hh, hd),
            cb_s[None], s_s[None])
```

```python
import functools
import math

import jax
import jax.numpy as jnp
from jax import lax
from jax.experimental import pallas as pl
from jax.experimental.pallas import tpu as pltpu

F32 = jnp.float32
BF16 = jnp.bfloat16

MOBA_HEADS = 8
MOBA_HEAD_DIM = 64
MOBA_WIDTH = MOBA_HEADS * MOBA_HEAD_DIM
MOBA_BLOCK = 256
MOBA_TOPK = 3
GDN_HEAD_DIM = 128
GDN_HEADS = 4
GDN_WIDTH = GDN_HEADS * GDN_HEAD_DIM
GDN_CONV = 4
GDN_CHUNK = 64
GDN_GROUP = 8
X_HEADS = 4
RMS_EPS = 1e-6
L2_EPS = 1e-6

LANES = 128
SUBLANES = 8
VMEM_LIMIT = 56 * 1024 * 1024
NEG = -1e30


def _params(sem):
    return pltpu.CompilerParams(dimension_semantics=sem, vmem_limit_bytes=VMEM_LIMIT)


def _const_spec(shape):
    nd = len(shape)
    return pl.BlockSpec(shape, lambda *_: (0,) * nd, pipeline_mode=pl.Buffered(1))


def _dot(a, b):
    return jnp.dot(a, b, preferred_element_type=F32)


def _dot_nt(a, b):
    return lax.dot_general(a, b, (((1,), (1,)), ((), ())), preferred_element_type=F32)


def _dot_tn(a, b):
    return lax.dot_general(a, b, (((0,), (0,)), ((), ())), preferred_element_type=F32)


def _split2(x):
    hi = x.astype(BF16)
    lo = (x - hi.astype(F32)).astype(BF16)
    return hi, lo


def _split3(x):
    hi = x.astype(BF16)
    r = x - hi.astype(F32)
    mid = r.astype(BF16)
    lo = (r - mid.astype(F32)).astype(BF16)
    return hi, mid, lo


def _dot3(a, b, dot=_dot):
    ah, al = _split2(a)
    bh, bl = _split2(b)
    return dot(ah, bh) + (dot(ah, bl) + dot(al, bh))


def _dot_exact_lhs(a_bf16, b, dot=_dot):
    bh, bm, bl = _split3(b)
    return dot(a_bf16, bh) + (dot(a_bf16, bm) + dot(a_bf16, bl))


def _rms(x, w):
    return x * lax.rsqrt(jnp.mean(x * x, axis=-1, keepdims=True) + RMS_EPS) * w


def _sigmoid(x):
    return 1.0 / (1.0 + jnp.exp(-x))


def _silu(x):
    return x * _sigmoid(x)


def _softplus(x):
    return jnp.maximum(x, 0.0) + jnp.log1p(jnp.exp(-jnp.abs(x)))


def _norm_matmul_kernel(*refs, splits, n_hi, chunk, head_major):
    x_ref, nw_ref, w_ref = refs[:3]
    if n_hi:
        wlo_ref, out_refs = refs[3], refs[4:]
    else:
        out_refs = refs[3:]
    xn = _rms(x_ref[...], nw_ref[...])
    xh = xn.astype(BF16)
    if n_hi:
        xl = (xn - xh.astype(F32)).astype(BF16)
    c0 = 0
    for oi, (o_ref, width) in enumerate(zip(out_refs, splits)):
        for s in range(0, width, chunk):
            wd = min(chunk, width - s)
            a, b = c0 + s, c0 + s + wd
            acc = _dot(xh, w_ref[:, a:b])
            if b <= n_hi:
                acc = acc + (_dot(xl, w_ref[:, a:b]) + _dot(xh, wlo_ref[:, a:b]))
            if oi in head_major:
                o_ref[0] = acc.T.reshape(o_ref.shape[1:])
            else:
                o_ref[:, s:s + wd] = acc
        c0 += width


def _norm_matmul(x, norm_w, w_hi, w_lo, splits, tm, head_major=(), seq_len=None, head_dim=None):
    m, k = x.shape
    n_hi = 0 if w_lo is None else w_lo.shape[1]
    assert m % tm == 0 and sum(splits) == w_hi.shape[1]
    in_specs = [pl.BlockSpec((tm, k), lambda i: (i, 0)), _const_spec((1, k)), _const_spec(w_hi.shape)]
    args = [x, norm_w.reshape(1, k), w_hi]
    if n_hi:
        in_specs.append(_const_spec(w_lo.shape))
        args.append(w_lo)
    chunk = 512
    out_specs, out_shape = [], []
    for oi, wd in enumerate(splits):
        if oi in head_major:
            assert wd <= chunk and wd % head_dim == 0 and seq_len % tm == 0 and m % seq_len == 0
            tiles = seq_len // tm
            out_specs.append(pl.BlockSpec((1, wd // head_dim, head_dim, tm), lambda i: (i // tiles, 0, 0, i % tiles)))
            out_shape.append(jax.ShapeDtypeStruct((m // seq_len, wd // head_dim, head_dim, seq_len), F32))
        else:
            out_specs.append(pl.BlockSpec((tm, wd), lambda i: (i, 0)))
            out_shape.append(jax.ShapeDtypeStruct((m, wd), F32))
    return pl.pallas_call(
        functools.partial(_norm_matmul_kernel, splits=tuple(splits), n_hi=n_hi, chunk=chunk,
                          head_major=tuple(head_major)),
        grid=(m // tm,),
        in_specs=in_specs,
        out_specs=out_specs,
        out_shape=out_shape,
        compiler_params=_params(("parallel",)),
        name="norm_matmul",
    )(*args)


def _topk_select(gate, valid, colf, ncol, axis):
    g = jnp.where(valid, gate, -jnp.inf)
    sel = None
    for _ in range(MOBA_TOPK):
        m = jnp.max(g, axis=axis, keepdims=True)
        idx = jnp.min(jnp.where(g == m, colf, float(ncol)), axis=axis, keepdims=True)
        pick = colf == idx
        sel = pick if sel is None else jnp.logical_or(sel, pick)
        g = jnp.where(pick, -jnp.inf, g)
    return jnp.logical_and(sel, valid)


AUG_ALIBI = 9
AUG_SEL0 = 16
V_ROWS = LANES + 16
LOG2E = 1.4426950408889634
MOBA_UNROLL = 4


def _moba_prompt_kernel(coef_ref, q_ref, k_ref, v_ref, o_ref, kaug_ref, vt_ref, kmp_ref, qat_ref, acc_ref,
                        st_ref, p_ref, al_ref, *, blk, nb):
    hp = pl.program_id(1)
    i = pl.program_id(2)
    d = MOBA_HEAD_DIM
    i_f = jnp.asarray(i, F32)

    @pl.when(i == 0)
    def _():
        kmp_ref[...] = jnp.zeros_like(kmp_ref)
        lane = lax.broadcasted_iota(jnp.int32, (blk, LANES), 1)
        rowf = lax.broadcasted_iota(jnp.int32, (blk, LANES), 0).astype(F32)

        for j in range(nb):
            cols = slice(j * blk, (j + 1) * blk)
            kj = k_ref[0, :, :, cols].reshape(LANES, blk).T
            kmp_ref[j:j + 1, :] = jnp.mean(kj, axis=0, keepdims=True)
            aug = jnp.where(lane < 3, float(j), jnp.where(lane < 6, 1.0, jnp.where(lane < AUG_ALIBI, rowf, 0.0)))
            aug = jnp.where(lane == AUG_SEL0 + j, 1.0, aug)
            kaug_ref[j, :, 0:LANES] = kj.astype(BF16)
            kaug_ref[j, :, LANES:2 * LANES] = aug.astype(BF16)
            vt_ref[j, 0:LANES, :] = v_ref[0, :, :, cols].reshape(LANES, blk).astype(BF16)
            vt_ref[j, LANES:V_ROWS, :] = jnp.ones((V_ROWS - LANES, blk), BF16)

    qt = q_ref[0].T
    row = lax.broadcasted_iota(jnp.int32, (LANES, blk), 0)
    nbr = kmp_ref.shape[0]
    brow = lax.broadcasted_iota(jnp.int32, (nbr, blk), 0)
    browf = brow.astype(F32)
    crow = lax.broadcasted_iota(jnp.int32, (AUG_SEL0, blk), 0)
    kmp = kmp_ref[...]
    for h2 in range(2):
        head = 2 * hp + h2
        qh = jnp.where((row >= d * h2) & (row < d * (h2 + 1)), qt, 0.0)
        gate = _dot3(kmp, qh)
        sel = _topk_select(gate, brow < i, browf, nbr, 0)
        selbias = jnp.where(jnp.logical_or(sel, brow == i), 0.0, NEG)
        coef = jnp.zeros((AUG_SEL0, blk), F32)
        c_blk = jnp.full((1, blk), -(coef_ref[head, 6] * i_f), F32)
        for li, part in zip((3, 4, 5), _split3(c_blk)):
            coef = jnp.where(crow == li, part.astype(F32), coef)
        for li, ci in ((0, 0), (1, 1), (2, 2), (6, 3), (7, 4), (8, 5)):
            coef = jnp.where(crow == li, coef_ref[head, ci], coef)
        qat_ref[h2, 0:LANES, :] = (qh * (d ** -0.5 * LOG2E)).astype(BF16)
        qat_ref[h2, LANES:LANES + AUG_SEL0, :] = coef.astype(BF16)
        qat_ref[h2, LANES + AUG_SEL0:LANES + AUG_SEL0 + nbr, :] = selbias.astype(BF16)
        qat_ref[h2, LANES + AUG_SEL0 + nbr:, :] = jnp.zeros((LANES - AUG_SEL0 - nbr, blk), BF16)

    def stage_a(slot, kb):
        for h2 in range(2):
            st_ref[slot, h2] = _dot(kaug_ref[kb], qat_ref[h2])

    def stage_b(slot, ms):
        new = []
        for h2 in range(2):
            st = st_ref[slot, h2]
            m_new = jnp.maximum(ms[h2], jnp.max(st, axis=0, keepdims=True))
            al_ref[slot, h2] = jnp.exp2(ms[h2] - m_new)
            p_ref[slot, h2] = jnp.exp2(st - m_new).astype(BF16)
            new.append(m_new)
        return tuple(new)

    def stage_c(slot, kb):
        for h2 in range(2):
            acc_ref[h2] = al_ref[slot, h2] * acc_ref[h2] + _dot(vt_ref[kb], p_ref[slot, h2])

    acc_ref[...] = jnp.zeros_like(acc_ref)
    p_ref[1] = jnp.zeros(p_ref.shape[1:], BF16)
    al_ref[1] = jnp.ones(al_ref.shape[1:], F32)
    keyi = lax.broadcasted_iota(jnp.int32, (blk, blk), 0)
    qryi = lax.broadcasted_iota(jnp.int32, (blk, blk), 1)
    for h2 in range(2):
        st_ref[0, h2] = jnp.where(keyi <= qryi, _dot(kaug_ref[i], qat_ref[h2]), NEG)
    ms = (jnp.full((1, blk), NEG, F32),) * 2

    def step(s, slot, ms):
        stage_c(slot, jnp.where(s == 2, i, jnp.maximum(s - 3, 0)))
        ms = stage_b(1 - slot, ms)
        stage_a(slot, s - 1)
        return ms

    def unrolled(t, ms):
        for u in range(MOBA_UNROLL):
            ms = step(MOBA_UNROLL * t + 1 + u, (1 + u) % 2, ms)
        return ms

    n_full = i // MOBA_UNROLL
    ms = lax.fori_loop(0, n_full, unrolled, ms)
    ms = lax.fori_loop(MOBA_UNROLL * n_full + 1, i + 1, lambda s, ms: step(s, s & 1, ms), ms)
    last = i & 1
    stage_c(1 - last, jnp.where(i == 1, i, jnp.maximum(i - 2, 0)))
    stage_b(last, ms)
    stage_c(last, jnp.maximum(i - 1, 0))

    outs = []
    for h2 in range(2):
        acc = acc_ref[h2]
        outs.append(acc[0:LANES, :] / acc[LANES:LANES + 1, :])
    o_ref[0] = jnp.where(row < d, outs[0], outs[1]).T


def _moba_prompt(q, kt, vt, slopes):
    b, t, w = q.shape
    blk = MOBA_BLOCK
    assert t % blk == 0 and w == MOBA_WIDTH and blk == 2 * LANES
    assert kt.shape == (b, MOBA_HEADS, MOBA_HEAD_DIM, t) and 2 * MOBA_HEAD_DIM == LANES
    nb = t // blk
    nbr = -(-nb // 16) * 16
    assert AUG_SEL0 + nbr < LANES and nb <= 256 and AUG_ALIBI <= AUG_SEL0 and AUG_SEL0 % 16 == 0
    c1 = slopes * LOG2E
    c256 = c1 * blk
    coef = jnp.stack([p.astype(F32) for p in _split3(c256)] + [p.astype(F32) for p in _split3(c1)]
                     + [c256, jnp.zeros_like(c1)], axis=1)
    kv_spec = pl.BlockSpec((1, 2, MOBA_HEAD_DIM, t), lambda bi, hp, i: (bi, hp, 0, 0))
    q_spec = pl.BlockSpec((1, blk, LANES), lambda bi, hp, i: (bi, i, hp))
    return pl.pallas_call(
        functools.partial(_moba_prompt_kernel, blk=blk, nb=nb),
        grid=(b, w // LANES, nb),
        in_specs=[pl.BlockSpec(memory_space=pltpu.SMEM), q_spec, kv_spec, kv_spec],
        out_specs=q_spec,
        out_shape=jax.ShapeDtypeStruct((b, t, w), F32),
        scratch_shapes=[pltpu.VMEM((nb, blk, 2 * LANES), BF16),
                        pltpu.VMEM((nb, V_ROWS, blk), BF16),
                        pltpu.VMEM((nbr, LANES), F32),
                        pltpu.VMEM((2, 2 * LANES, blk), BF16),
                        pltpu.VMEM((2, V_ROWS, blk), F32),
                        pltpu.VMEM((2, 2, blk, blk), F32),
                        pltpu.VMEM((2, 2, blk, blk), BF16),
                        pltpu.VMEM((2, 2, 1, blk), F32)],
        compiler_params=_params(("parallel", "parallel", "arbitrary")),
        name="moba_prompt",
    )(coef, q, kt, vt)


def _tri_inverses(lows, c):
    row = lax.broadcasted_iota(jnp.int32, (c, c), 0)
    col = lax.broadcasted_iota(jnp.int32, (c, c), 1)
    eye = jnp.where(row == col, 1.0, 0.0)
    ts = [eye - low for low in lows]
    ps = [low.astype(BF16) for low in lows]
    for _ in range(int(math.log2(c)) - 1):
        ps = [_dot(p, p).astype(BF16) for p in ps]
        ts = [t + _dot(t.astype(BF16), p) for t, p in zip(ts, ps)]
    return ts


def _gdn_prompt_kernel(x_ref, z_ref, ab_ref, cw_ref, alog_ref, dtb_ref, nw_ref,
                       o_ref, s_out_ref, cb_out_ref,
                       xext_ref, s_ref, qn_ref, kn_ref, vv_ref, gc_ref, beta_ref,
                       el_ref, m_ref, n_ref, qe_ref, o0_ref, *, tt, c):
    t_idx = pl.program_id(1)
    nt = pl.num_programs(1)
    hd = GDN_HEAD_DIM
    gw = GDN_WIDTH
    halo = SUBLANES

    @pl.when(t_idx == 0)
    def _():
        s_ref[...] = jnp.zeros_like(s_ref)
        xext_ref[0:halo, :] = jnp.zeros((halo, 3 * gw), F32)

    xext_ref[halo:halo + tt, :] = x_ref[0]
    conv = xext_ref[halo:halo + tt, :] * cw_ref[GDN_CONV - 1:GDN_CONV, :]
    for j in range(GDN_CONV - 1):
        off = halo - (GDN_CONV - 1) + j
        conv = conv + xext_ref[off:off + tt, :] * cw_ref[j:j + 1, :]

    @pl.when(t_idx == nt - 1)
    def _():
        cb_out_ref[0] = xext_ref[halo + tt - (GDN_CONV - 1):halo + tt, :]

    xext_ref[0:halo, :] = xext_ref[tt:tt + halo, :]

    act = _silu(conv)
    for h in range(GDN_HEADS):
        qh = act[:, h * hd:(h + 1) * hd]
        kh = act[:, gw + h * hd:gw + (h + 1) * hd]
        qn_ref[:, h * hd:(h + 1) * hd] = qh * lax.rsqrt(jnp.sum(qh * qh, axis=-1, keepdims=True) + L2_EPS) * (hd ** -0.5)
        kn_ref[:, h * hd:(h + 1) * hd] = kh * lax.rsqrt(jnp.sum(kh * kh, axis=-1, keepdims=True) + L2_EPS)
    vv_ref[...] = act[:, 2 * gw:]

    abv = ab_ref[0]
    g_all = -jnp.exp(alog_ref[...]) * _softplus(abv + dtb_ref[...])
    beta_ref[...] = _sigmoid(abv)
    rc = lax.broadcasted_iota(jnp.int32, (c, c), 0)
    cc = lax.broadcasted_iota(jnp.int32, (c, c), 1)
    tri = jnp.where(rc >= cc, 1.0, 0.0).astype(BF16)
    for ci in range(tt // c):
        gc_ref[ci * c:(ci + 1) * c, :] = _dot_exact_lhs(tri, g_all[ci * c:(ci + 1) * c, :])

    incl = rc >= cc
    strict = rc > cc
    lane8 = lax.broadcasted_iota(jnp.int32, (SUBLANES, LANES), 1)
    nw = nw_ref[...]

    def prepare_group(gi, carry):
        probs = []
        for cj in range(GDN_GROUP):
            ci = gi * GDN_GROUP + cj
            r0 = pl.multiple_of(ci * c, c)
            gcs = gc_ref[pl.ds(r0, c), :]
            bet = beta_ref[pl.ds(r0, c), :]
            e_gc = jnp.exp(gcs)
            g_last = gcs[c - 1:c, :]
            e_rev = jnp.exp(g_last - gcs)
            el_ref[pl.ds(ci, 1), :] = jnp.exp(g_last)
            for h in range(GDN_HEADS):
                hs = slice(h * hd, (h + 1) * hd)
                onehot = jnp.where(lane8 == h, 1.0, 0.0).astype(BF16)
                grow = _dot_exact_lhs(onehot, gcs, _dot_nt)[0:1, :]
                probs.append(dict(
                    idx=ci * GDN_HEADS + h, qn=qn_ref[pl.ds(r0, c), hs], kn=kn_ref[pl.ds(r0, c), hs],
                    vv=vv_ref[pl.ds(r0, c), hs], bcol=bet[:, GDN_HEADS + h:GDN_HEADS + h + 1],
                    egc=e_gc[:, h:h + 1], erev=e_rev[:, h:h + 1],
                    decay=jnp.where(incl, jnp.exp(gcs[:, h:h + 1] - grow), 0.0)))
        for p in probs:
            p["knb"] = p["kn"].astype(BF16)
        lows = [jnp.where(strict, p["bcol"] * _dot_nt(p["knb"], p["knb"]) * p["decay"], 0.0) for p in probs]
        tinvs = _tri_inverses(lows, c)
        sols = []
        for p, tinv in zip(probs, tinvs):
            rhs = jnp.concatenate([p["kn"] * (p["bcol"] * p["egc"]), p["vv"] * p["bcol"]], axis=1)
            sols.append(_dot(tinv.astype(BF16), rhs.astype(BF16)).astype(BF16))
        for p, sol in zip(probs, sols):
            aqk = (_dot_nt(p["qn"].astype(BF16), p["knb"]) * p["decay"]).astype(BF16)
            upd = _dot_tn((p["kn"] * p["erev"]).astype(BF16), sol)
            out = _dot(aqk, sol)
            m_ref[p["idx"]] = upd[:, 0:hd].astype(BF16)
            n_ref[p["idx"]] = upd[:, hd:]
            qe_ref[p["idx"]] = (p["qn"] * p["egc"] - out[:, 0:hd]).astype(BF16)
            o0_ref[p["idx"]] = out[:, hd:]
        return carry

    lax.fori_loop(0, tt // (GDN_GROUP * c), prepare_group, 0)

    def advance(ci, carry):
        r0 = pl.multiple_of(ci * c, c)
        e_last = el_ref[pl.ds(ci, 1), :]
        for h in range(GDN_HEADS):
            hs = slice(h * hd, (h + 1) * hd)
            idx = ci * GDN_HEADS + h
            s = s_ref[h]
            sb = s.astype(BF16)
            o = _dot(qe_ref[idx], sb) + o0_ref[idx]
            s_ref[h] = s * e_last[:, h:h + 1] - _dot(m_ref[idx], sb) + n_ref[idx]
            o = o * lax.rsqrt(jnp.mean(o * o, axis=-1, keepdims=True) + RMS_EPS) * nw
            o_ref[0, pl.ds(r0, c), hs] = o * _silu(z_ref[0, pl.ds(r0, c), hs])
        return carry

    lax.fori_loop(0, tt // c, advance, 0)

    @pl.when(t_idx == nt - 1)
    def _():
        s_out_ref[0] = s_ref[...]


def _gdn_prompt(gqkv, z, ab, conv_w, alog_pad, dtb_pad, norm_w, tt):
    b, t, w3 = gqkv.shape
    c = GDN_CHUNK
    assert t % tt == 0 and tt % (GDN_GROUP * c) == 0 and w3 == 3 * GDN_WIDTH
    hd = GDN_HEAD_DIM
    nch = tt // c
    tile = lambda wd: pl.BlockSpec((1, tt, wd), lambda bi, ti: (bi, ti, 0))
    return pl.pallas_call(
        functools.partial(_gdn_prompt_kernel, tt=tt, c=c),
        grid=(b, t // tt),
        in_specs=[tile(w3), tile(GDN_WIDTH), tile(LANES), _const_spec(conv_w.shape),
                  _const_spec((1, LANES)), _const_spec((1, LANES)), _const_spec((1, hd))],
        out_specs=[tile(GDN_WIDTH),
                   pl.BlockSpec((1, GDN_HEADS, hd, hd), lambda bi, ti: (bi, 0, 0, 0)),
                   pl.BlockSpec((1, GDN_CONV - 1, w3), lambda bi, ti: (bi, 0, 0))],
        out_shape=[jax.ShapeDtypeStruct((b, t, GDN_WIDTH), F32),
                   jax.ShapeDtypeStruct((b, GDN_HEADS, hd, hd), F32),
                   jax.ShapeDtypeStruct((b, GDN_CONV - 1, w3), F32)],
        scratch_shapes=[pltpu.VMEM((tt + SUBLANES, w3), F32),
                        pltpu.VMEM((GDN_HEADS, hd, hd), F32),
                        pltpu.VMEM((tt, GDN_WIDTH), F32),
                        pltpu.VMEM((tt, GDN_WIDTH), F32),
                        pltpu.VMEM((tt, GDN_WIDTH), F32),
                        pltpu.VMEM((tt, LANES), F32),
                        pltpu.VMEM((tt, LANES), F32),
                        pltpu.VMEM((nch, LANES), F32),
                        pltpu.VMEM((nch * GDN_HEADS, hd, hd), BF16),
                        pltpu.VMEM((nch * GDN_HEADS, hd, hd), F32),
                        pltpu.VMEM((nch * GDN_HEADS, c, hd), BF16),
                        pltpu.VMEM((nch * GDN_HEADS, c, hd), F32)],
        compiler_params=_params(("parallel", "arbitrary")),
        name="gdn_prompt",
    )(gqkv, z, ab, conv_w, alog_pad, dtb_pad, norm_w.reshape(1, hd))


def _gdn_sample_kernel(x_ref, z_ref, ab_ref, cb_ref, s_in_ref, cw_ref, alog_ref, dtb_ref, nw_ref,
                       o_ref, s_out_ref, cb_out_ref):
    hd = GDN_HEAD_DIM
    gw = GDN_WIDTH
    nc = GDN_CONV - 1
    x = x_ref[0]
    cb = cb_ref[0]
    conv = x * cw_ref[nc:nc + 1, :]
    for j in range(nc):
        conv = conv + cb[j:j + 1, :] * cw_ref[j:j + 1, :]
    cb_out_ref[0, 0:nc - 1, :] = cb[1:nc, :]
    cb_out_ref[0, nc - 1:nc, :] = x
    act = _silu(conv)
    abv = ab_ref[0]
    g_all = -jnp.exp(alog_ref[...]) * _softplus(abv + dtb_ref[...])
    e_g = jnp.exp(g_all)
    beta = _sigmoid(abv)
    z = z_ref[0]
    nw = nw_ref[...]
    eye = (lax.broadcasted_iota(jnp.int32, (hd, hd), 0) == lax.broadcasted_iota(jnp.int32, (hd, hd), 1))

    def as_column(row):
        return jnp.sum(jnp.where(eye, row, 0.0), axis=1, keepdims=True)

    for h in range(GDN_HEADS):
        hs = slice(h * hd, (h + 1) * hd)
        qh = act[:, hs]
        kh = act[:, gw + h * hd:gw + (h + 1) * hd]
        vv = act[:, 2 * gw + h * hd:2 * gw + (h + 1) * hd]
        qn = qh * lax.rsqrt(jnp.sum(qh * qh, axis=-1, keepdims=True) + L2_EPS) * (hd ** -0.5)
        kn = kh * lax.rsqrt(jnp.sum(kh * kh, axis=-1, keepdims=True) + L2_EPS)
        eg = e_g[:, h:h + 1]
        bh = beta[:, GDN_HEADS + h:GDN_HEADS + h + 1]
        s = s_in_ref[0, h]
        k_col = as_column(kn)
        v_new = vv * bh - jnp.sum((k_col * (bh * eg)) * s, axis=0, keepdims=True)
        qk = jnp.sum(qn * kn, axis=-1, keepdims=True)
        o = jnp.sum((as_column(qn) * eg) * s, axis=0, keepdims=True) + qk * v_new
        s_out_ref[0, h] = s * eg + k_col * v_new
        o = o * lax.rsqrt(jnp.mean(o * o, axis=-1, keepdims=True) + RMS_EPS) * nw
        o_ref[0, :, hs] = o * _silu(z[:, hs])


def _gdn_sample(gqkv, z, ab, conv_buf, s0, conv_w, alog_pad, dtb_pad, norm_w):
    db, w3 = gqkv.shape
    hd = GDN_HEAD_DIM
    nc = GDN_CONV - 1
    row = lambda wd: pl.BlockSpec((1, 1, wd), lambda bi: (bi, 0, 0))
    st = pl.BlockSpec((1, GDN_HEADS, hd, hd), lambda bi: (bi, 0, 0, 0))
    cbs = pl.BlockSpec((1, nc, w3), lambda bi: (bi, 0, 0))
    o, s_new, cb_new = pl.pallas_call(
        _gdn_sample_kernel,
        grid=(db,),
        in_specs=[row(w3), row(GDN_WIDTH), row(LANES), cbs, st, _const_spec(conv_w.shape),
                  _const_spec((1, LANES)), _const_spec((1, LANES)), _const_spec((1, hd))],
        out_specs=[row(GDN_WIDTH), st, cbs],
        out_shape=[jax.ShapeDtypeStruct((db, 1, GDN_WIDTH), F32),
                   jax.ShapeDtypeStruct((db, GDN_HEADS, hd, hd), F32),
                   jax.ShapeDtypeStruct((db, nc, w3), F32)],
        compiler_params=_params(("parallel",)),
        name="gdn_sample",
    )(gqkv.reshape(db, 1, w3), z.reshape(db, 1, GDN_WIDTH), ab.reshape(db, 1, LANES), conv_buf, s0,
      conv_w, alog_pad, dtb_pad, norm_w.reshape(1, hd))
    return o.reshape(db, GDN_WIDTH), s_new, cb_new


DMA_RING = 8
ROUTE_BANKS = 4


def _moba_route_kernel(pt_ref, q_ref, pool_ref, o_ref, kbuf, sem, qc_ref, *, n_pages, ppb):
    b = pl.program_id(0)
    n_seq = pl.num_programs(0)
    n_groups = n_pages // DMA_RING
    bpg = DMA_RING // ppb
    nblk = n_pages // ppb
    hh, d = MOBA_HEADS, MOBA_HEAD_DIM

    def page_copy(seq, g, u, bank):
        page = pt_ref[seq * n_pages + g * DMA_RING + u]
        return pltpu.make_async_copy(pool_ref.at[page], kbuf.at[bank, u], sem.at[bank, u])

    def start_group(seq, g, bank):
        for u in range(DMA_RING):
            page_copy(seq, g, u, bank).start()

    ahead = ROUTE_BANKS - 1

    @pl.when(b == 0)
    def _():
        for g0 in range(ahead):
            start_group(b, g0, g0)

    q = q_ref[0]
    eye = lax.broadcasted_iota(jnp.int32, (d, d), 0) == lax.broadcasted_iota(jnp.int32, (d, d), 1)
    for h in range(hh):
        q_col = jnp.sum(jnp.where(eye, q[h:h + 1, :], 0.0), axis=1, keepdims=True)
        qc_ref[h] = jnp.broadcast_to(q_col, (d, LANES))
    lane = lax.broadcasted_iota(jnp.int32, (hh, LANES), 1)
    rowi = lax.broadcasted_iota(jnp.int32, (hh, LANES), 0)

    def group(g, gate):
        bank = g & (ROUTE_BANKS - 1)
        for u in range(DMA_RING):
            page_copy(b, g, u, bank).wait()

        free_bank = (g + ahead) & (ROUTE_BANKS - 1)

        @pl.when(g + ahead < n_groups)
        def _():
            start_group(b, g + ahead, free_bank)

        @pl.when(jnp.logical_and(g + ahead >= n_groups, b + 1 < n_seq))
        def _():
            start_group(b + 1, g + ahead - n_groups, free_bank)

        for kb in range(bpg):
            x = kbuf[bank, kb * ppb]
            for u in range(1, ppb):
                x = x + kbuf[bank, kb * ppb + u]
            w = (x * qc_ref[...]).reshape(hh, d // SUBLANES, SUBLANES, LANES)
            r = jnp.sum(w, axis=1)
            per_head = jnp.zeros((hh, LANES), F32)
            for h in range(hh):
                per_head = jnp.where(rowi == h, jnp.sum(r[h], axis=0, keepdims=True), per_head)
            total = jnp.sum(per_head, axis=1, keepdims=True)
            gate = jnp.where(lane == g * bpg + kb, total, gate)
        return gate

    gate = lax.fori_loop(0, n_groups, group, jnp.zeros((hh, LANES), F32)) * (1.0 / MOBA_BLOCK)
    lanef = lane.astype(F32)
    gate = jnp.where(lane < nblk, gate, -jnp.inf)
    out = jnp.zeros((hh, LANES), F32)
    for r in range(MOBA_TOPK):
        m = jnp.max(gate, axis=1, keepdims=True)
        idx = jnp.min(jnp.where(gate == m, lanef, float(LANES)), axis=1, keepdims=True)
        gate = jnp.where(lanef == idx, -jnp.inf, gate)
        out = jnp.where(lane == r, idx, out)
    o_ref[0] = out.astype(jnp.int32)


def _moba_route(q3, pool_kt, pt_flat, n_pages, ppb):
    db = q3.shape[0]
    _, hh, d, ps = pool_kt.shape
    assert n_pages % (ROUTE_BANKS * DMA_RING) == 0 and ROUTE_BANKS & (ROUTE_BANKS - 1) == 0
    assert DMA_RING % ppb == 0 and ps == LANES and n_pages // ppb <= LANES
    return pl.pallas_call(
        functools.partial(_moba_route_kernel, n_pages=n_pages, ppb=ppb),
        grid_spec=pltpu.PrefetchScalarGridSpec(
            num_scalar_prefetch=1,
            grid=(db,),
            in_specs=[pl.BlockSpec((1, hh, d), lambda b, pt: (b, 0, 0)), pl.BlockSpec(memory_space=pl.ANY)],
            out_specs=pl.BlockSpec((1, hh, LANES), lambda b, pt: (b, 0, 0)),
            scratch_shapes=[pltpu.VMEM((ROUTE_BANKS, DMA_RING, hh, d, ps), F32),
                            pltpu.SemaphoreType.DMA((ROUTE_BANKS, DMA_RING)),
                            pltpu.VMEM((hh, d, LANES), F32)],
        ),
        out_shape=jax.ShapeDtypeStruct((db, hh, LANES), jnp.int32),
        compiler_params=_params(("arbitrary",)),
        name="moba_sample_route",
    )(pt_flat, q3, pool_kt)


def _moba_sample_attn_kernel(pg_ref, bk_ref, slope_ref, q_ref, kn_ref, vn_ref, pk_ref, pv_ref, o_ref,
                             kbuf, vbuf, ksem, vsem, *, per_head, ppb, ps, past):
    b = pl.program_id(0)
    n_seq = pl.num_programs(0)
    hh, d = MOBA_HEADS, MOBA_HEAD_DIM
    n_slab = hh * per_head

    def copies(seq, t, bank):
        pg = pg_ref[seq * n_slab + t]
        h = t // per_head
        return (pltpu.make_async_copy(pk_ref.at[pg, h], kbuf.at[bank, t], ksem.at[bank, t]),
                pltpu.make_async_copy(pv_ref.at[pg, h], vbuf.at[bank, t], vsem.at[bank, t]))

    def start_all(seq, bank):
        for t in range(n_slab):
            for c in copies(seq, t, bank):
                c.start()

    @pl.when(b == 0)
    def _():
        start_all(b, 0)

    bank = b & 1
    for t in range(n_slab):
        for c in copies(b, t, bank):
            c.wait()

    @pl.when(b + 1 < n_seq)
    def _():
        start_all(b + 1, 1 - bank)

    q = q_ref[0] * (d ** -0.5)
    kn = kn_ref[0]
    vn = vn_ref[0]
    lanef = lax.broadcasted_iota(jnp.int32, (1, ps), 1).astype(F32)
    for h in range(hh):
        qh = jnp.broadcast_to(q[h:h + 1, :], (SUBLANES, d)).astype(BF16)
        rows = []
        for j in range(per_head):
            t = h * per_head + j
            blk = bk_ref[b * (n_slab // ppb) + t // ppb]
            pos0 = blk * (ppb * ps) + (t % ppb) * ps
            s = _dot(qh, kbuf[bank, t].astype(BF16))[0:1, :]
            rows.append(s - slope_ref[h] * (jnp.asarray(past - pos0, F32) - lanef))
        s_all = jnp.concatenate(rows, axis=1)
        s_new = jnp.sum(q[h:h + 1, :] * kn[h:h + 1, :], axis=1, keepdims=True)
        m = jnp.maximum(jnp.max(s_all, axis=1, keepdims=True), s_new)
        p = jnp.exp(s_all - m)
        p_new = jnp.exp(s_new - m)
        l = jnp.sum(p, axis=1, keepdims=True) + p_new
        v_all = jnp.concatenate([vbuf[bank, h * per_head + j].astype(BF16) for j in range(per_head)], axis=1)
        o = _dot_nt(jnp.broadcast_to(p, (SUBLANES, p.shape[1])).astype(BF16), v_all)[0:1, :]
        o_ref[0, h:h + 1, :] = (o + p_new * vn[h:h + 1, :]) / l


def _moba_sample_attn(q3, kn3, vn3, pool_kt, pool_vt, pages_flat, blocks_flat, slopes, per_head, ppb, past):
    db, hh, d = q3.shape
    _, _, _, ps = pool_kt.shape
    n_slab = hh * per_head
    row = pl.BlockSpec((1, hh, d), lambda b, pg, bk: (b, 0, 0))
    hbm = pl.BlockSpec(memory_space=pl.ANY)
    return pl.pallas_call(
        functools.partial(_moba_sample_attn_kernel, per_head=per_head, ppb=ppb, ps=ps, past=past),
        grid_spec=pltpu.PrefetchScalarGridSpec(
            num_scalar_prefetch=2,
            grid=(db,),
            in_specs=[pl.BlockSpec(memory_space=pltpu.SMEM), row, row, row, hbm, hbm],
            out_specs=row,
            scratch_shapes=[pltpu.VMEM((2, n_slab, d, ps), F32), pltpu.VMEM((2, n_slab, d, ps), F32),
                            pltpu.SemaphoreType.DMA((2, n_slab)), pltpu.SemaphoreType.DMA((2, n_slab))],
        ),
        out_shape=jax.ShapeDtypeStruct((db, hh, d), F32),
        compiler_params=_params(("arbitrary",)),
        name="moba_sample_attn",
    )(pages_flat, blocks_flat, slopes, q3, kn3, vn3, pool_kt, pool_vt)


def _moba_sample(q, k_new, v_new, pool_k, pool_v, page_table, slopes):
    db, n_pages = page_table.shape
    _, ps, hh, d = pool_k.shape
    assert MOBA_BLOCK % ps == 0 and hh == MOBA_HEADS and d == MOBA_HEAD_DIM
    ppb = MOBA_BLOCK // ps
    past = n_pages * ps
    assert past % MOBA_BLOCK == 0 and past // MOBA_BLOCK >= MOBA_TOPK
    pool_kt = jnp.transpose(pool_k, (0, 2, 3, 1))
    pool_vt = jnp.transpose(pool_v, (0, 2, 3, 1))
    q3 = q.reshape(db, hh, d)
    blocks = _moba_route(q3, pool_kt, page_table.reshape(-1), n_pages, ppb)[:, :, :MOBA_TOPK]
    pidx = (blocks[..., None] * ppb + jnp.arange(ppb, dtype=jnp.int32)).reshape(db, hh * MOBA_TOPK * ppb)
    pages = jnp.take_along_axis(page_table, pidx, axis=1)
    o = _moba_sample_attn(q3, k_new.reshape(db, hh, d), v_new.reshape(db, hh, d), pool_kt, pool_vt,
                          pages.reshape(-1), blocks.reshape(-1), slopes, MOBA_TOPK * ppb, ppb, past)
    return o.reshape(db, hh * d)


def _mid_kernel(om_ref, og_ref, x_ref, wo_ref, nw_ref, wq_ref, h_ref, q_ref):
    half = om_ref.shape[1]
    h = x_ref[...] + (_dot(om_ref[...].astype(BF16), wo_ref[0:half, :])
                      + _dot(og_ref[...].astype(BF16), wo_ref[half:, :]))
    h_ref[...] = h
    q_ref[...] = _dot(_rms(h, nw_ref[...]).astype(BF16), wq_ref[...])


def _mid(o_moba, o_gdn, x, w_out, norm_w, w_xq, tm):
    m, dm = x.shape
    half = o_moba.shape[1]
    assert m % tm == 0
    tile = lambda wd: pl.BlockSpec((tm, wd), lambda i: (i, 0))
    return pl.pallas_call(
        _mid_kernel,
        grid=(m // tm,),
        in_specs=[tile(half), tile(o_gdn.shape[1]), tile(dm), _const_spec(w_out.shape), _const_spec((1, dm)),
                  _const_spec(w_xq.shape)],
        out_specs=[tile(dm), tile(w_xq.shape[1])],
        out_shape=[jax.ShapeDtypeStruct((m, dm), F32), jax.ShapeDtypeStruct((m, w_xq.shape[1]), F32)],
        compiler_params=_params(("parallel",)),
        name="out_proj_xq",
    )(o_moba, o_gdn, x, w_out, norm_w.reshape(1, dm), w_xq)


def _xattn_prompt_kernel(q_ref, mk_ref, mv_ref, o_ref, *, hd):
    for h in range(X_HEADS):
        hs = slice(h * hd, (h + 1) * hd)
        qh = (q_ref[0, :, hs] * (hd ** -0.5)).astype(BF16)
        s = _dot_nt(qh, mk_ref[0, :, hs].astype(BF16))
        m = jnp.max(s, axis=1, keepdims=True)
        p = jnp.exp(s - m)
        l = jnp.sum(p, axis=1, keepdims=True)
        o_ref[0, :, hs] = _dot(p.astype(BF16), mv_ref[0, :, hs].astype(BF16)) / l


def _xattn_prompt(qx, mk, mv, tm):
    b, t, w = qx.shape
    nm = mk.shape[1]
    assert t % tm == 0
    q_spec = pl.BlockSpec((1, tm, w), lambda bi, ti: (bi, ti, 0))
    m_spec = pl.BlockSpec((1, nm, w), lambda bi, ti: (bi, 0, 0))
    return pl.pallas_call(
        functools.partial(_xattn_prompt_kernel, hd=w // X_HEADS),
        grid=(b, t // tm),
        in_specs=[q_spec, m_spec, m_spec],
        out_specs=q_spec,
        out_shape=jax.ShapeDtypeStruct((b, t, w), F32),
        compiler_params=_params(("parallel", "parallel")),
        name="xattn_prompt",
    )(qx, mk, mv)


def _xattn_sample_kernel(q_ref, mk_ref, mv_ref, o_ref, *, hd):
    for h in range(X_HEADS):
        hs = slice(h * hd, (h + 1) * hd)
        qh = q_ref[0, :, hs] * (hd ** -0.5)
        s = jnp.sum(mk_ref[0, :, hs] * qh, axis=-1, keepdims=True)
        m = jnp.max(s, axis=0, keepdims=True)
        p = jnp.exp(s - m)
        l = jnp.sum(p, axis=0, keepdims=True)
        o_ref[0, :, hs] = jnp.sum(p * mv_ref[0, :, hs], axis=0, keepdims=True) / l


def _xattn_sample(qx, mk, mv):
    db, w = qx.shape
    nm = mk.shape[1]
    q_spec = pl.BlockSpec((1, 1, w), lambda bi: (bi, 0, 0))
    m_spec = pl.BlockSpec((1, nm, w), lambda bi: (bi, 0, 0))
    return pl.pallas_call(
        functools.partial(_xattn_sample_kernel, hd=w // X_HEADS),
        grid=(db,),
        in_specs=[q_spec, m_spec, m_spec],
        out_specs=q_spec,
        out_shape=jax.ShapeDtypeStruct((db, 1, w), F32),
        compiler_params=_params(("parallel",)),
        name="xattn_sample",
    )(qx.reshape(db, 1, w), mk, mv).reshape(db, w)


def _tail_kernel(h_ref, ox_ref, wxo_ref, nf_ref, wgu_ref, wd_ref, fn_ref, y_ref, *, d_ff, chunk):
    h = h_ref[...] + _dot(ox_ref[...].astype(BF16), wxo_ref[...])
    hn = _rms(h, nf_ref[...]).astype(BF16)
    acc = jnp.zeros(h.shape, F32)
    for c0 in range(0, d_ff, chunk):
        g = _dot(hn, wgu_ref[:, c0:c0 + chunk])
        u = _dot(hn, wgu_ref[:, d_ff + c0:d_ff + c0 + chunk])
        acc = acc + _dot((_silu(g) * u).astype(BF16), wd_ref[c0:c0 + chunk, :])
    y_ref[...] = _rms(h + acc, fn_ref[...])


def _tail(h, ox, w_xo, norm_ffn_w, w_gu, w_down, final_norm_w, tm):
    m, dm = h.shape
    d_ff = w_down.shape[0]
    chunk = 2 * LANES
    assert m % tm == 0 and d_ff % chunk == 0
    tile = lambda wd: pl.BlockSpec((tm, wd), lambda i: (i, 0))
    return pl.pallas_call(
        functools.partial(_tail_kernel, d_ff=d_ff, chunk=chunk),
        grid=(m // tm,),
        in_specs=[tile(dm), tile(ox.shape[1]), _const_spec(w_xo.shape), _const_spec((1, dm)),
                  _const_spec(w_gu.shape), _const_spec(w_down.shape), _const_spec((1, dm))],
        out_specs=tile(dm),
        out_shape=jax.ShapeDtypeStruct((m, dm), F32),
        compiler_params=_params(("parallel",)),
        name="xo_swiglu_norm",
    )(h, ox, w_xo, norm_ffn_w.reshape(1, dm), w_gu, w_down, final_norm_w.reshape(1, dm))


def _row_tile(m, pref):
    return pref if m % pref == 0 else m


def kernel(x_prompt, x_sample, cache_k, cache_v, page_table, state_conv, state_gdn, cache_mem_k, cache_mem_v, mem_prompt, norm_mix_w, w_in, conv_w, a_log, dt_bias, gdn_norm_w, w_out, norm_x_w, mem_norm_w, w_xq, w_xkv, w_xo, norm_ffn_w, w_gu, w_down, final_norm_w):
    assert w_in.shape[0] == 1, "one layer"
    b, t, dm = x_prompt.shape
    db, ds, _ = x_sample.shape
    assert ds == 1
    mw, gw = MOBA_WIDTH, GDN_WIDTH
    n_in = 3 * mw + 4 * gw + 2 * GDN_HEADS
    assert w_in.shape[2] == n_in

    w_in_p = jnp.pad(w_in[0], ((0, 0), (0, 3 * mw + 4 * gw + LANES - n_in)))
    w_in_hi = w_in_p.astype(BF16)
    w_in_lo = (w_in_p[:, :2 * mw] - w_in_hi[:, :2 * mw].astype(F32)).astype(BF16)
    in_splits = (mw, mw, mw, 3 * gw, gw, LANES)
    w_out_b, w_xq_b, w_xo_b = w_out[0].astype(BF16), w_xq[0].astype(BF16), w_xo[0].astype(BF16)
    w_gu_b, w_down_b, w_xkv_b = w_gu[0].astype(BF16), w_down[0].astype(BF16), w_xkv[0].astype(BF16)
    xw = w_xq.shape[2]
    pad_h = lambda v: jnp.pad(v.astype(F32), (0, LANES - GDN_HEADS)).reshape(1, LANES)
    alog_pad, dtb_pad = pad_h(a_log[0]), pad_h(dt_bias[0])
    slopes = jnp.exp2(-8.0 * jnp.arange(1, MOBA_HEADS + 1, dtype=F32) / MOBA_HEADS)

    mp = b * t
    xp = x_prompt.reshape(mp, dm)
    qm, kt, vt, gqkv, z, ab = _norm_matmul(xp, norm_mix_w[0], w_in_hi, w_in_lo, in_splits, _row_tile(t, 512),
                                           head_major=(1, 2), seq_len=t, head_dim=MOBA_HEAD_DIM)
    mk_p, mv_p = _norm_matmul(mem_prompt.reshape(-1, dm), mem_norm_w[0], w_xkv_b, None, (xw, xw),
                              _row_tile(mem_prompt.shape[0] * mem_prompt.shape[1], 512))
    n_mem = mem_prompt.shape[1]
    o_moba = _moba_prompt(qm.reshape(b, t, mw), kt, vt, slopes)
    o_gdn, s_p, cb_p = _gdn_prompt(gqkv.reshape(b, t, 3 * gw), z.reshape(b, t, gw), ab.reshape(b, t, LANES),
                                   conv_w[0], alog_pad, dtb_pad, gdn_norm_w[0], _row_tile(t, 512))
    h1, qx = _mid(o_moba.reshape(mp, mw), o_gdn.reshape(mp, gw), xp, w_out_b, norm_x_w[0], w_xq_b, _row_tile(mp, 512))
    ox = _xattn_prompt(qx.reshape(b, t, xw), mk_p.reshape(b, n_mem, xw), mv_p.reshape(b, n_mem, xw), _row_tile(t, 512))
    y_p = _tail(h1, ox.reshape(mp, xw), w_xo_b, norm_ffn_w[0], w_gu_b, w_down_b, final_norm_w, _row_tile(mp, 512))

    xs = x_sample.reshape(db, dm)
    qs, ks, vs, gqkv_s, z_s, ab_s = _norm_matmul(xs, norm_mix_w[0], w_in_hi, w_in_lo, in_splits, db)
    o_moba_s = _moba_sample(qs, ks, vs, cache_k[0], cache_v[0], page_table, slopes)
    o_gdn_s, s_s, cb_s = _gdn_sample(gqkv_s, z_s, ab_s, state_conv[0], state_gdn[0], conv_w[0], alog_pad, dtb_pad,
                                     gdn_norm_w[0])
    h1_s, qx_s = _mid(o_moba_s, o_gdn_s, xs, w_out_b, norm_x_w[0], w_xq_b, db)
    ox_s = _xattn_sample(qx_s, cache_mem_k[0].reshape(db, -1, xw), cache_mem_v[0].reshape(db, -1, xw))
    y_s = _tail(h1_s, ox_s, w_xo_b, norm_ffn_w[0], w_gu_b, w_down_b, final_norm_w, db)

    hh, hd = MOBA_HEADS, MOBA_HEAD_DIM
    xh = X_HEADS
    return (y_p.reshape(b, t, dm), y_s.reshape(db, 1, dm),
            jnp.transpose(kt, (0, 3, 1, 2))[None], jnp.transpose(vt, (0, 3, 1, 2))[None],
            cb_p[None], s_p[None],
            mk_p.reshape(1, b, n_mem, xh, xw // xh), mv_p.reshape(1, b, n_mem, xh, xw // xh),
            ks.reshape(1, db, 1, hh, hd), vs.reshape(1, db, 1, hh, hd),
            cb_s[None], s_s[None])
```

```python
import functools
import math

import jax
import jax.numpy as jnp
from jax import lax
from jax.experimental import pallas as pl
from jax.experimental.pallas import tpu as pltpu

F32 = jnp.float32
BF16 = jnp.bfloat16

MOBA_HEADS = 8
MOBA_HEAD_DIM = 64
MOBA_WIDTH = MOBA_HEADS * MOBA_HEAD_DIM
MOBA_BLOCK = 256
MOBA_TOPK = 3
GDN_HEAD_DIM = 128
GDN_HEADS = 4
GDN_WIDTH = GDN_HEADS * GDN_HEAD_DIM
GDN_CONV = 4
GDN_CHUNK = 64
GDN_GROUP = 8
X_HEADS = 4
RMS_EPS = 1e-6
L2_EPS = 1e-6

LANES = 128
SUBLANES = 8
VMEM_LIMIT = 56 * 1024 * 1024
NEG = -1e30


def _params(sem):
    return pltpu.CompilerParams(dimension_semantics=sem, vmem_limit_bytes=VMEM_LIMIT)


def _const_spec(shape):
    nd = len(shape)
    return pl.BlockSpec(shape, lambda *_: (0,) * nd, pipeline_mode=pl.Buffered(1))


def _dot(a, b):
    return jnp.dot(a, b, preferred_element_type=F32)


def _dot_nt(a, b):
    return lax.dot_general(a, b, (((1,), (1,)), ((), ())), preferred_element_type=F32)


def _dot_tn(a, b):
    return lax.dot_general(a, b, (((0,), (0,)), ((), ())), preferred_element_type=F32)


def _split2(x):
    hi = x.astype(BF16)
    lo = (x - hi.astype(F32)).astype(BF16)
    return hi, lo


def _split3(x):
    hi = x.astype(BF16)
    r = x - hi.astype(F32)
    mid = r.astype(BF16)
    lo = (r - mid.astype(F32)).astype(BF16)
    return hi, mid, lo


def _dot3(a, b, dot=_dot):
    ah, al = _split2(a)
    bh, bl = _split2(b)
    return dot(ah, bh) + (dot(ah, bl) + dot(al, bh))


def _dot_exact_lhs(a_bf16, b, dot=_dot):
    bh, bm, bl = _split3(b)
    return dot(a_bf16, bh) + (dot(a_bf16, bm) + dot(a_bf16, bl))


def _rms(x, w):
    return x * lax.rsqrt(jnp.mean(x * x, axis=-1, keepdims=True) + RMS_EPS) * w


def _sigmoid(x):
    return 1.0 / (1.0 + jnp.exp(-x))


def _silu(x):
    return x * _sigmoid(x)


def _softplus(x):
    return jnp.maximum(x, 0.0) + jnp.log1p(jnp.exp(-jnp.abs(x)))


def _norm_matmul_kernel(*refs, splits, n_hi, chunk, head_major):
    x_ref, nw_ref, w_ref = refs[:3]
    if n_hi:
        wlo_ref, out_refs = refs[3], refs[4:]
    else:
        out_refs = refs[3:]
    xn = _rms(x_ref[...], nw_ref[...])
    xh = xn.astype(BF16)
    if n_hi:
        xl = (xn - xh.astype(F32)).astype(BF16)
    c0 = 0
    for oi, (o_ref, width) in enumerate(zip(out_refs, splits)):
        for s in range(0, width, chunk):
            wd = min(chunk, width - s)
            a, b = c0 + s, c0 + s + wd
            acc = _dot(xh, w_ref[:, a:b])
            if b <= n_hi:
                acc = acc + (_dot(xl, w_ref[:, a:b]) + _dot(xh, wlo_ref[:, a:b]))
            if oi in head_major:
                o_ref[0] = acc.T.reshape(o_ref.shape[1:])
            else:
                o_ref[:, s:s + wd] = acc
        c0 += width


def _norm_matmul(x, norm_w, w_hi, w_lo, splits, tm, head_major=(), seq_len=None, head_dim=None):
    m, k = x.shape
    n_hi = 0 if w_lo is None else w_lo.shape[1]
    assert m % tm == 0 and sum(splits) == w_hi.shape[1]
    in_specs = [pl.BlockSpec((tm, k), lambda i: (i, 0)), _const_spec((1, k)), _const_spec(w_hi.shape)]
    args = [x, norm_w.reshape(1, k), w_hi]
    if n_hi:
        in_specs.append(_const_spec(w_lo.shape))
        args.append(w_lo)
    chunk = 512
    out_specs, out_shape = [], []
    for oi, wd in enumerate(splits):
        if oi in head_major:
            assert wd <= chunk and wd % head_dim == 0 and seq_len % tm == 0 and m % seq_len == 0
            tiles = seq_len // tm
            out_specs.append(pl.BlockSpec((1, wd // head_dim, head_dim, tm), lambda i: (i // tiles, 0, 0, i % tiles)))
            out_shape.append(jax.ShapeDtypeStruct((m // seq_len, wd // head_dim, head_dim, seq_len), F32))
        else:
            out_specs.append(pl.BlockSpec((tm, wd), lambda i: (i, 0)))
            out_shape.append(jax.ShapeDtypeStruct((m, wd), F32))
    return pl.pallas_call(
        functools.partial(_norm_matmul_kernel, splits=tuple(splits), n_hi=n_hi, chunk=chunk,
                          head_major=tuple(head_major)),
        grid=(m // tm,),
        in_specs=in_specs,
        out_specs=out_specs,
        out_shape=out_shape,
        compiler_params=_params(("parallel",)),
        name="norm_matmul",
    )(*args)


def _topk_select(gates, valid, colf, ncol, axis):
    gs = [jnp.where(valid, gate, -jnp.inf) for gate in gates]
    sels = [None] * len(gs)
    for _ in range(MOBA_TOPK):
        ms = [jnp.max(g, axis=axis, keepdims=True) for g in gs]
        idxs = [jnp.min(jnp.where(g == m, colf, float(ncol)), axis=axis, keepdims=True) for g, m in zip(gs, ms)]
        picks = [colf == idx for idx in idxs]
        sels = [pick if sel is None else jnp.logical_or(sel, pick) for sel, pick in zip(sels, picks)]
        gs = [jnp.where(pick, -jnp.inf, g) for g, pick in zip(gs, picks)]
    return [jnp.logical_and(sel, valid) for sel in sels]


AUG_ALIBI = 9
AUG_SEL0 = 16
V_ROWS = LANES + 16
LOG2E = 1.4426950408889634
MOBA_UNROLL = 4


def _moba_prompt_kernel(coef_ref, q_ref, k_ref, v_ref, o_ref, kaug_ref, vt_ref, kmp_ref, qat_ref, acc_ref,
                        st_ref, p_ref, al_ref, *, blk, nb):
    hp = pl.program_id(1)
    i = pl.program_id(2)
    d = MOBA_HEAD_DIM
    i_f = jnp.asarray(i, F32)

    @pl.when(i == 0)
    def _():
        kmp_ref[...] = jnp.zeros_like(kmp_ref)
        lane = lax.broadcasted_iota(jnp.int32, (blk, LANES), 1)
        rowf = lax.broadcasted_iota(jnp.int32, (blk, LANES), 0).astype(F32)

        for j in range(nb):
            cols = slice(j * blk, (j + 1) * blk)
            kj = k_ref[0, :, :, cols].reshape(LANES, blk).T
            kmp_ref[j:j + 1, :] = jnp.mean(kj, axis=0, keepdims=True)
            aug = jnp.where(lane < 3, float(j), jnp.where(lane < 6, 1.0, jnp.where(lane < AUG_ALIBI, rowf, 0.0)))
            aug = jnp.where(lane == AUG_SEL0 + j, 1.0, aug)
            kaug_ref[j, :, 0:LANES] = kj.astype(BF16)
            kaug_ref[j, :, LANES:2 * LANES] = aug.astype(BF16)
            vt_ref[j, 0:LANES, :] = v_ref[0, :, :, cols].reshape(LANES, blk).astype(BF16)
            vt_ref[j, LANES:V_ROWS, :] = jnp.ones((V_ROWS - LANES, blk), BF16)

    qt = q_ref[0].T
    row = lax.broadcasted_iota(jnp.int32, (LANES, blk), 0)
    nbr = kmp_ref.shape[0]
    brow = lax.broadcasted_iota(jnp.int32, (nbr, blk), 0)
    browf = brow.astype(F32)
    crow = lax.broadcasted_iota(jnp.int32, (AUG_SEL0, blk), 0)
    kmp = kmp_ref[...]
    qhs = [jnp.where((row >= d * h2) & (row < d * (h2 + 1)), qt, 0.0) for h2 in range(2)]
    gates = [_dot3(kmp, qh) for qh in qhs]
    sels = _topk_select(gates, brow < i, browf, nbr, 0)
    for h2 in range(2):
        head = 2 * hp + h2
        selbias = jnp.where(jnp.logical_or(sels[h2], brow == i), 0.0, NEG)
        coef = jnp.zeros((AUG_SEL0, blk), F32)
        c_blk = jnp.full((1, blk), -(coef_ref[head, 6] * i_f), F32)
        for li, part in zip((3, 4, 5), _split3(c_blk)):
            coef = jnp.where(crow == li, part.astype(F32), coef)
        for li, ci in ((0, 0), (1, 1), (2, 2), (6, 3), (7, 4), (8, 5)):
            coef = jnp.where(crow == li, coef_ref[head, ci], coef)
        qat_ref[h2, 0:LANES, :] = (qhs[h2] * (d ** -0.5 * LOG2E)).astype(BF16)
        qat_ref[h2, LANES:LANES + AUG_SEL0, :] = coef.astype(BF16)
        qat_ref[h2, LANES + AUG_SEL0:LANES + AUG_SEL0 + nbr, :] = selbias.astype(BF16)
        qat_ref[h2, LANES + AUG_SEL0 + nbr:, :] = jnp.zeros((LANES - AUG_SEL0 - nbr, blk), BF16)

    def stage_a(slot, kb):
        for h2 in range(2):
            st_ref[slot, h2] = _dot(kaug_ref[kb], qat_ref[h2])

    def stage_b(slot, ms):
        new = []
        for h2 in range(2):
            st = st_ref[slot, h2]
            m_new = jnp.maximum(ms[h2], jnp.max(st, axis=0, keepdims=True))
            al_ref[slot, h2] = jnp.exp2(ms[h2] - m_new)
            p_ref[slot, h2] = jnp.exp2(st - m_new).astype(BF16)
            new.append(m_new)
        return tuple(new)

    def stage_c(slot, kb):
        for h2 in range(2):
            acc_ref[h2] = al_ref[slot, h2] * acc_ref[h2] + _dot(vt_ref[kb], p_ref[slot, h2])

    acc_ref[...] = jnp.zeros_like(acc_ref)
    p_ref[1] = jnp.zeros(p_ref.shape[1:], BF16)
    al_ref[1] = jnp.ones(al_ref.shape[1:], F32)
    keyi = lax.broadcasted_iota(jnp.int32, (blk, blk), 0)
    qryi = lax.broadcasted_iota(jnp.int32, (blk, blk), 1)
    for h2 in range(2):
        st_ref[0, h2] = jnp.where(keyi <= qryi, _dot(kaug_ref[i], qat_ref[h2]), NEG)
    ms = (jnp.full((1, blk), NEG, F32),) * 2

    def step(s, slot, ms):
        stage_c(slot, jnp.where(s == 2, i, jnp.maximum(s - 3, 0)))
        ms = stage_b(1 - slot, ms)
        stage_a(slot, s - 1)
        return ms

    def unrolled(t, ms):
        for u in range(MOBA_UNROLL):
            ms = step(MOBA_UNROLL * t + 1 + u, (1 + u) % 2, ms)
        return ms

    def pair(t, ms):
        return step(t + 1, 0, step(t, 1, ms))

    n_full = i // MOBA_UNROLL
    ms = lax.fori_loop(0, n_full, unrolled, ms)
    done = MOBA_UNROLL * n_full + 1
    n_pair = (i + 1 - done) // 2
    ms = lax.fori_loop(0, n_pair, lambda t, ms: pair(done + 2 * t, ms), ms)
    ms = lax.fori_loop(done + 2 * n_pair, i + 1, lambda s, ms: step(s, s & 1, ms), ms)
    last = i & 1
    stage_c(1 - last, jnp.where(i == 1, i, jnp.maximum(i - 2, 0)))
    stage_b(last, ms)
    stage_c(last, jnp.maximum(i - 1, 0))

    outs = []
    for h2 in range(2):
        acc = acc_ref[h2]
        outs.append(acc[0:LANES, :] / acc[LANES:LANES + 1, :])
    o_ref[0] = jnp.where(row < d, outs[0], outs[1]).T


def _moba_prompt(q, kt, vt, slopes):
    b, t, w = q.shape
    blk = MOBA_BLOCK
    assert t % blk == 0 and w == MOBA_WIDTH and blk == 2 * LANES
    assert kt.shape == (b, MOBA_HEADS, MOBA_HEAD_DIM, t) and 2 * MOBA_HEAD_DIM == LANES
    nb = t // blk
    nbr = -(-nb // 16) * 16
    assert AUG_SEL0 + nbr < LANES and nb <= 256 and AUG_ALIBI <= AUG_SEL0 and AUG_SEL0 % 16 == 0
    c1 = slopes * LOG2E
    c256 = c1 * blk
    coef = jnp.stack([p.astype(F32) for p in _split3(c256)] + [p.astype(F32) for p in _split3(c1)]
                     + [c256, jnp.zeros_like(c1)], axis=1)
    kv_spec = pl.BlockSpec((1, 2, MOBA_HEAD_DIM, t), lambda bi, hp, i: (bi, hp, 0, 0))
    q_spec = pl.BlockSpec((1, blk, LANES), lambda bi, hp, i: (bi, i, hp))
    return pl.pallas_call(
        functools.partial(_moba_prompt_kernel, blk=blk, nb=nb),
        grid=(b, w // LANES, nb),
        in_specs=[pl.BlockSpec(memory_space=pltpu.SMEM), q_spec, kv_spec, kv_spec],
        out_specs=q_spec,
        out_shape=jax.ShapeDtypeStruct((b, t, w), F32),
        scratch_shapes=[pltpu.VMEM((nb, blk, 2 * LANES), BF16),
                        pltpu.VMEM((nb, V_ROWS, blk), BF16),
                        pltpu.VMEM((nbr, LANES), F32),
                        pltpu.VMEM((2, 2 * LANES, blk), BF16),
                        pltpu.VMEM((2, V_ROWS, blk), F32),
                        pltpu.VMEM((2, 2, blk, blk), F32),
                        pltpu.VMEM((2, 2, blk, blk), BF16),
                        pltpu.VMEM((2, 2, 1, blk), F32)],
        compiler_params=_params(("parallel", "parallel", "arbitrary")),
        name="moba_prompt",
    )(coef, q, kt, vt)


def _tri_inverses(lows, c):
    row = lax.broadcasted_iota(jnp.int32, (c, c), 0)
    col = lax.broadcasted_iota(jnp.int32, (c, c), 1)
    eye = jnp.where(row == col, 1.0, 0.0)
    ts = [eye - low for low in lows]
    ps = [low.astype(BF16) for low in lows]
    for _ in range(int(math.log2(c)) - 1):
        ps = [_dot(p, p).astype(BF16) for p in ps]
        ts = [t + _dot(t.astype(BF16), p) for t, p in zip(ts, ps)]
    return ts


def _gdn_prompt_kernel(x_ref, z_ref, ab_ref, cw_ref, alog_ref, dtb_ref, nw_ref,
                       o_ref, s_out_ref, cb_out_ref,
                       xext_ref, s_ref, qn_ref, kn_ref, vv_ref, gc_ref, beta_ref,
                       el_ref, m_ref, n_ref, qe_ref, o0_ref, *, tt, c):
    t_idx = pl.program_id(1)
    nt = pl.num_programs(1)
    hd = GDN_HEAD_DIM
    gw = GDN_WIDTH
    halo = SUBLANES

    @pl.when(t_idx == 0)
    def _():
        s_ref[...] = jnp.zeros_like(s_ref)
        xext_ref[0:halo, :] = jnp.zeros((halo, 3 * gw), F32)

    xext_ref[halo:halo + tt, :] = x_ref[0]
    conv = xext_ref[halo:halo + tt, :] * cw_ref[GDN_CONV - 1:GDN_CONV, :]
    for j in range(GDN_CONV - 1):
        off = halo - (GDN_CONV - 1) + j
        conv = conv + xext_ref[off:off + tt, :] * cw_ref[j:j + 1, :]

    @pl.when(t_idx == nt - 1)
    def _():
        cb_out_ref[0] = xext_ref[halo + tt - (GDN_CONV - 1):halo + tt, :]

    xext_ref[0:halo, :] = xext_ref[tt:tt + halo, :]

    act = _silu(conv)
    for h in range(GDN_HEADS):
        qh = act[:, h * hd:(h + 1) * hd]
        kh = act[:, gw + h * hd:gw + (h + 1) * hd]
        qn_ref[:, h * hd:(h + 1) * hd] = qh * lax.rsqrt(jnp.sum(qh * qh, axis=-1, keepdims=True) + L2_EPS) * (hd ** -0.5)
        kn_ref[:, h * hd:(h + 1) * hd] = kh * lax.rsqrt(jnp.sum(kh * kh, axis=-1, keepdims=True) + L2_EPS)
    vv_ref[...] = act[:, 2 * gw:]

    abv = ab_ref[0]
    g_all = -jnp.exp(alog_ref[...]) * _softplus(abv + dtb_ref[...])
    beta_ref[...] = _sigmoid(abv)
    rc = lax.broadcasted_iota(jnp.int32, (c, c), 0)
    cc = lax.broadcasted_iota(jnp.int32, (c, c), 1)
    tri = jnp.where(rc >= cc, 1.0, 0.0).astype(BF16)
    for ci in range(tt // c):
        gc_ref[ci * c:(ci + 1) * c, :] = _dot_exact_lhs(tri, g_all[ci * c:(ci + 1) * c, :])

    incl = rc >= cc
    strict = rc > cc
    lane8 = lax.broadcasted_iota(jnp.int32, (SUBLANES, LANES), 1)
    nw = nw_ref[...]

    def prepare_group(gi, carry):
        probs = []
        for cj in range(GDN_GROUP):
            ci = gi * GDN_GROUP + cj
            r0 = pl.multiple_of(ci * c, c)
            gcs = gc_ref[pl.ds(r0, c), :]
            bet = beta_ref[pl.ds(r0, c), :]
            e_gc = jnp.exp(gcs)
            g_last = gcs[c - 1:c, :]
            e_rev = jnp.exp(g_last - gcs)
            el_ref[pl.ds(ci, 1), :] = jnp.exp(g_last)
            for h in range(GDN_HEADS):
                hs = slice(h * hd, (h + 1) * hd)
                onehot = jnp.where(lane8 == h, 1.0, 0.0).astype(BF16)
                grow = _dot_exact_lhs(onehot, gcs, _dot_nt)[0:1, :]
                probs.append(dict(
                    idx=ci * GDN_HEADS + h, qn=qn_ref[pl.ds(r0, c), hs], kn=kn_ref[pl.ds(r0, c), hs],
                    vv=vv_ref[pl.ds(r0, c), hs], bcol=bet[:, GDN_HEADS + h:GDN_HEADS + h + 1],
                    egc=e_gc[:, h:h + 1], erev=e_rev[:, h:h + 1],
                    decay=jnp.where(incl, jnp.exp(gcs[:, h:h + 1] - grow), 0.0)))
        for p in probs:
            p["knb"] = p["kn"].astype(BF16)
        lows = [jnp.where(strict, p["bcol"] * _dot_nt(p["knb"], p["knb"]) * p["decay"], 0.0) for p in probs]
        tinvs = _tri_inverses(lows, c)
        sols = []
        for p, tinv in zip(probs, tinvs):
            rhs = jnp.concatenate([p["kn"] * (p["bcol"] * p["egc"]), p["vv"] * p["bcol"]], axis=1)
            sols.append(_dot(tinv.astype(BF16), rhs.astype(BF16)).astype(BF16))
        for p, sol in zip(probs, sols):
            aqk = (_dot_nt(p["qn"].astype(BF16), p["knb"]) * p["decay"]).astype(BF16)
            upd = _dot_tn((p["kn"] * p["erev"]).astype(BF16), sol)
            out = _dot(aqk, sol)
            m_ref[p["idx"]] = upd[:, 0:hd].astype(BF16)
            n_ref[p["idx"]] = upd[:, hd:]
            qe_ref[p["idx"]] = (p["qn"] * p["egc"] - out[:, 0:hd]).astype(BF16)
            o0_ref[p["idx"]] = out[:, hd:]
        return carry

    lax.fori_loop(0, tt // (GDN_GROUP * c), prepare_group, 0)

    def advance(ci, carry):
        r0 = pl.multiple_of(ci * c, c)
        e_last = el_ref[pl.ds(ci, 1), :]
        for h in range(GDN_HEADS):
            hs = slice(h * hd, (h + 1) * hd)
            idx = ci * GDN_HEADS + h
            s = s_ref[h]
            sb = s.astype(BF16)
            o = _dot(qe_ref[idx], sb) + o0_ref[idx]
            s_ref[h] = s * e_last[:, h:h + 1] - _dot(m_ref[idx], sb) + n_ref[idx]
            o = o * lax.rsqrt(jnp.mean(o * o, axis=-1, keepdims=True) + RMS_EPS) * nw
            o_ref[0, pl.ds(r0, c), hs] = o * _silu(z_ref[0, pl.ds(r0, c), hs])
        return carry

    lax.fori_loop(0, tt // c, advance, 0)

    @pl.when(t_idx == nt - 1)
    def _():
        s_out_ref[0] = s_ref[...]


def _gdn_prompt(gqkv, z, ab, conv_w, alog_pad, dtb_pad, norm_w, tt):
    b, t, w3 = gqkv.shape
    c = GDN_CHUNK
    assert t % tt == 0 and tt % (GDN_GROUP * c) == 0 and w3 == 3 * GDN_WIDTH
    hd = GDN_HEAD_DIM
    nch = tt // c
    tile = lambda wd: pl.BlockSpec((1, tt, wd), lambda bi, ti: (bi, ti, 0))
    return pl.pallas_call(
        functools.partial(_gdn_prompt_kernel, tt=tt, c=c),
        grid=(b, t // tt),
        in_specs=[tile(w3), tile(GDN_WIDTH), tile(LANES), _const_spec(conv_w.shape),
                  _const_spec((1, LANES)), _const_spec((1, LANES)), _const_spec((1, hd))],
        out_specs=[tile(GDN_WIDTH),
                   pl.BlockSpec((1, GDN_HEADS, hd, hd), lambda bi, ti: (bi, 0, 0, 0)),
                   pl.BlockSpec((1, GDN_CONV - 1, w3), lambda bi, ti: (bi, 0, 0))],
        out_shape=[jax.ShapeDtypeStruct((b, t, GDN_WIDTH), F32),
                   jax.ShapeDtypeStruct((b, GDN_HEADS, hd, hd), F32),
                   jax.ShapeDtypeStruct((b, GDN_CONV - 1, w3), F32)],
        scratch_shapes=[pltpu.VMEM((tt + SUBLANES, w3), F32),
                        pltpu.VMEM((GDN_HEADS, hd, hd), F32),
                        pltpu.VMEM((tt, GDN_WIDTH), F32),
                        pltpu.VMEM((tt, GDN_WIDTH), F32),
                        pltpu.VMEM((tt, GDN_WIDTH), F32),
                        pltpu.VMEM((tt, LANES), F32),
                        pltpu.VMEM((tt, LANES), F32),
                        pltpu.VMEM((nch, LANES), F32),
                        pltpu.VMEM((nch * GDN_HEADS, hd, hd), BF16),
                        pltpu.VMEM((nch * GDN_HEADS, hd, hd), F32),
                        pltpu.VMEM((nch * GDN_HEADS, c, hd), BF16),
                        pltpu.VMEM((nch * GDN_HEADS, c, hd), F32)],
        compiler_params=_params(("parallel", "arbitrary")),
        name="gdn_prompt",
    )(gqkv, z, ab, conv_w, alog_pad, dtb_pad, norm_w.reshape(1, hd))


def _gdn_sample_kernel(x_ref, z_ref, ab_ref, cb_ref, s_in_ref, cw_ref, alog_ref, dtb_ref, nw_ref,
                       o_ref, s_out_ref, cb_out_ref):
    hd = GDN_HEAD_DIM
    gw = GDN_WIDTH
    nc = GDN_CONV - 1
    x = x_ref[0]
    cb = cb_ref[0]
    conv = x * cw_ref[nc:nc + 1, :]
    for j in range(nc):
        conv = conv + cb[j:j + 1, :] * cw_ref[j:j + 1, :]
    cb_out_ref[0, 0:nc - 1, :] = cb[1:nc, :]
    cb_out_ref[0, nc - 1:nc, :] = x
    act = _silu(conv)
    abv = ab_ref[0]
    g_all = -jnp.exp(alog_ref[...]) * _softplus(abv + dtb_ref[...])
    e_g = jnp.exp(g_all)
    beta = _sigmoid(abv)
    z = z_ref[0]
    nw = nw_ref[...]
    eye = (lax.broadcasted_iota(jnp.int32, (hd, hd), 0) == lax.broadcasted_iota(jnp.int32, (hd, hd), 1))

    def as_column(row):
        return jnp.sum(jnp.where(eye, row, 0.0), axis=1, keepdims=True)

    for h in range(GDN_HEADS):
        hs = slice(h * hd, (h + 1) * hd)
        qh = act[:, hs]
        kh = act[:, gw + h * hd:gw + (h + 1) * hd]
        vv = act[:, 2 * gw + h * hd:2 * gw + (h + 1) * hd]
        qn = qh * lax.rsqrt(jnp.sum(qh * qh, axis=-1, keepdims=True) + L2_EPS) * (hd ** -0.5)
        kn = kh * lax.rsqrt(jnp.sum(kh * kh, axis=-1, keepdims=True) + L2_EPS)
        eg = e_g[:, h:h + 1]
        bh = beta[:, GDN_HEADS + h:GDN_HEADS + h + 1]
        s = s_in_ref[0, h]
        k_col = as_column(kn)
        v_new = vv * bh - jnp.sum((k_col * (bh * eg)) * s, axis=0, keepdims=True)
        qk = jnp.sum(qn * kn, axis=-1, keepdims=True)
        o = jnp.sum((as_column(qn) * eg) * s, axis=0, keepdims=True) + qk * v_new
        s_out_ref[0, h] = s * eg + k_col * v_new
        o = o * lax.rsqrt(jnp.mean(o * o, axis=-1, keepdims=True) + RMS_EPS) * nw
        o_ref[0, :, hs] = o * _silu(z[:, hs])


def _gdn_sample(gqkv, z, ab, conv_buf, s0, conv_w, alog_pad, dtb_pad, norm_w):
    db, w3 = gqkv.shape
    hd = GDN_HEAD_DIM
    nc = GDN_CONV - 1
    row = lambda wd: pl.BlockSpec((1, 1, wd), lambda bi: (bi, 0, 0))
    st = pl.BlockSpec((1, GDN_HEADS, hd, hd), lambda bi: (bi, 0, 0, 0))
    cbs = pl.BlockSpec((1, nc, w3), lambda bi: (bi, 0, 0))
    o, s_new, cb_new = pl.pallas_call(
        _gdn_sample_kernel,
        grid=(db,),
        in_specs=[row(w3), row(GDN_WIDTH), row(LANES), cbs, st, _const_spec(conv_w.shape),
                  _const_spec((1, LANES)), _const_spec((1, LANES)), _const_spec((1, hd))],
        out_specs=[row(GDN_WIDTH), st, cbs],
        out_shape=[jax.ShapeDtypeStruct((db, 1, GDN_WIDTH), F32),
                   jax.ShapeDtypeStruct((db, GDN_HEADS, hd, hd), F32),
                   jax.ShapeDtypeStruct((db, nc, w3), F32)],
        compiler_params=_params(("parallel",)),
        name="gdn_sample",
    )(gqkv.reshape(db, 1, w3), z.reshape(db, 1, GDN_WIDTH), ab.reshape(db, 1, LANES), conv_buf, s0,
      conv_w, alog_pad, dtb_pad, norm_w.reshape(1, hd))
    return o.reshape(db, GDN_WIDTH), s_new, cb_new


DMA_RING = 8
ROUTE_BANKS = 4


def _moba_route_kernel(pt_ref, q_ref, pool_ref, o_ref, kbuf, sem, qc_ref, *, n_pages, ppb):
    b = pl.program_id(0)
    n_seq = pl.num_programs(0)
    n_groups = n_pages // DMA_RING
    bpg = DMA_RING // ppb
    nblk = n_pages // ppb
    hh, d = MOBA_HEADS, MOBA_HEAD_DIM

    def page_copy(seq, g, u, bank):
        page = pt_ref[seq * n_pages + g * DMA_RING + u]
        return pltpu.make_async_copy(pool_ref.at[page], kbuf.at[bank, u], sem.at[bank, u])

    def start_group(seq, g, bank):
        for u in range(DMA_RING):
            page_copy(seq, g, u, bank).start()

    ahead = ROUTE_BANKS - 1

    @pl.when(b == 0)
    def _():
        for g0 in range(ahead):
            start_group(b, g0, g0)

    q = q_ref[0]
    eye = lax.broadcasted_iota(jnp.int32, (d, d), 0) == lax.broadcasted_iota(jnp.int32, (d, d), 1)
    for h in range(hh):
        q_col = jnp.sum(jnp.where(eye, q[h:h + 1, :], 0.0), axis=1, keepdims=True)
        qc_ref[h] = jnp.broadcast_to(q_col, (d, LANES))
    lane = lax.broadcasted_iota(jnp.int32, (hh, LANES), 1)
    rowi = lax.broadcasted_iota(jnp.int32, (hh, LANES), 0)

    def group(g, gate):
        bank = g & (ROUTE_BANKS - 1)
        for u in range(DMA_RING):
            page_copy(b, g, u, bank).wait()

        free_bank = (g + ahead) & (ROUTE_BANKS - 1)

        @pl.when(g + ahead < n_groups)
        def _():
            start_group(b, g + ahead, free_bank)

        @pl.when(jnp.logical_and(g + ahead >= n_groups, b + 1 < n_seq))
        def _():
            start_group(b + 1, g + ahead - n_groups, free_bank)

        for kb in range(bpg):
            x = kbuf[bank, kb * ppb]
            for u in range(1, ppb):
                x = x + kbuf[bank, kb * ppb + u]
            w = (x * qc_ref[...]).reshape(hh, d // SUBLANES, SUBLANES, LANES)
            r = jnp.sum(w, axis=1)
            per_head = jnp.zeros((hh, LANES), F32)
            for h in range(hh):
                per_head = jnp.where(rowi == h, jnp.sum(r[h], axis=0, keepdims=True), per_head)
            total = jnp.sum(per_head, axis=1, keepdims=True)
            gate = jnp.where(lane == g * bpg + kb, total, gate)
        return gate

    gate = lax.fori_loop(0, n_groups, group, jnp.zeros((hh, LANES), F32)) * (1.0 / MOBA_BLOCK)
    lanef = lane.astype(F32)
    gate = jnp.where(lane < nblk, gate, -jnp.inf)
    out = jnp.zeros((hh, LANES), F32)
    for r in range(MOBA_TOPK):
        m = jnp.max(gate, axis=1, keepdims=True)
        idx = jnp.min(jnp.where(gate == m, lanef, float(LANES)), axis=1, keepdims=True)
        gate = jnp.where(lanef == idx, -jnp.inf, gate)
        out = jnp.where(lane == r, idx, out)
    o_ref[0] = out.astype(jnp.int32)


def _moba_route(q3, pool_kt, pt_flat, n_pages, ppb):
    db = q3.shape[0]
    _, hh, d, ps = pool_kt.shape
    assert n_pages % (ROUTE_BANKS * DMA_RING) == 0 and ROUTE_BANKS & (ROUTE_BANKS - 1) == 0
    assert DMA_RING % ppb == 0 and ps == LANES and n_pages // ppb <= LANES
    return pl.pallas_call(
        functools.partial(_moba_route_kernel, n_pages=n_pages, ppb=ppb),
        grid_spec=pltpu.PrefetchScalarGridSpec(
            num_scalar_prefetch=1,
            grid=(db,),
            in_specs=[pl.BlockSpec((1, hh, d), lambda b, pt: (b, 0, 0)), pl.BlockSpec(memory_space=pl.ANY)],
            out_specs=pl.BlockSpec((1, hh, LANES), lambda b, pt: (b, 0, 0)),
            scratch_shapes=[pltpu.VMEM((ROUTE_BANKS, DMA_RING, hh, d, ps), F32),
                            pltpu.SemaphoreType.DMA((ROUTE_BANKS, DMA_RING)),
                            pltpu.VMEM((hh, d, LANES), F32)],
        ),
        out_shape=jax.ShapeDtypeStruct((db, hh, LANES), jnp.int32),
        compiler_params=_params(("arbitrary",)),
        name="moba_sample_route",
    )(pt_flat, q3, pool_kt)


def _moba_sample_attn_kernel(pg_ref, bk_ref, slope_ref, q_ref, kn_ref, vn_ref, pk_ref, pv_ref, o_ref,
                             kbuf, vbuf, ksem, vsem, *, per_head, ppb, ps, past):
    b = pl.program_id(0)
    n_seq = pl.num_programs(0)
    hh, d = MOBA_HEADS, MOBA_HEAD_DIM
    n_slab = hh * per_head

    def copies(seq, t, bank):
        pg = pg_ref[seq * n_slab + t]
        h = t // per_head
        return (pltpu.make_async_copy(pk_ref.at[pg, h], kbuf.at[bank, t], ksem.at[bank, t]),
                pltpu.make_async_copy(pv_ref.at[pg, h], vbuf.at[bank, t], vsem.at[bank, t]))

    def start_all(seq, bank):
        for t in range(n_slab):
            for c in copies(seq, t, bank):
                c.start()

    @pl.when(b == 0)
    def _():
        start_all(b, 0)

    bank = b & 1
    for t in range(n_slab):
        for c in copies(b, t, bank):
            c.wait()

    @pl.when(b + 1 < n_seq)
    def _():
        start_all(b + 1, 1 - bank)

    q = q_ref[0] * (d ** -0.5)
    kn = kn_ref[0]
    vn = vn_ref[0]
    lanef = lax.broadcasted_iota(jnp.int32, (1, ps), 1).astype(F32)
    qhs = [jnp.broadcast_to(q[h:h + 1, :], (SUBLANES, d)).astype(BF16) for h in range(hh)]
    scores = [_dot(qhs[t // per_head], kbuf[bank, t].astype(BF16))[0:1, :] for t in range(n_slab)]
    s_alls, s_news = [], []
    for h in range(hh):
        rows = []
        for j in range(per_head):
            t = h * per_head + j
            blk = bk_ref[b * (n_slab // ppb) + t // ppb]
            pos0 = blk * (ppb * ps) + (t % ppb) * ps
            rows.append(scores[t] - slope_ref[h] * (jnp.asarray(past - pos0, F32) - lanef))
        s_alls.append(jnp.concatenate(rows, axis=1))
        s_news.append(jnp.sum(q[h:h + 1, :] * kn[h:h + 1, :], axis=1, keepdims=True))
    ms = [jnp.maximum(jnp.max(s_all, axis=1, keepdims=True), s_new) for s_all, s_new in zip(s_alls, s_news)]
    ps_ = [jnp.exp(s_all - m) for s_all, m in zip(s_alls, ms)]
    p_news = [jnp.exp(s_new - m) for s_new, m in zip(s_news, ms)]
    outs = []
    for h in range(hh):
        v_all = jnp.concatenate([vbuf[bank, h * per_head + j].astype(BF16) for j in range(per_head)], axis=1)
        p8 = jnp.broadcast_to(ps_[h], (SUBLANES, ps_[h].shape[1])).astype(BF16)
        outs.append(_dot_nt(p8, v_all)[0:1, :])
    for h in range(hh):
        l = jnp.sum(ps_[h], axis=1, keepdims=True) + p_news[h]
        o_ref[0, h:h + 1, :] = (outs[h] + p_news[h] * vn[h:h + 1, :]) / l


def _moba_sample_attn(q3, kn3, vn3, pool_kt, pool_vt, pages_flat, blocks_flat, slopes, per_head, ppb, past):
    db, hh, d = q3.shape
    _, _, _, ps = pool_kt.shape
    n_slab = hh * per_head
    row = pl.BlockSpec((1, hh, d), lambda b, pg, bk: (b, 0, 0))
    hbm = pl.BlockSpec(memory_space=pl.ANY)
    return pl.pallas_call(
        functools.partial(_moba_sample_attn_kernel, per_head=per_head, ppb=ppb, ps=ps, past=past),
        grid_spec=pltpu.PrefetchScalarGridSpec(
            num_scalar_prefetch=2,
            grid=(db,),
            in_specs=[pl.BlockSpec(memory_space=pltpu.SMEM), row, row, row, hbm, hbm],
            out_specs=row,
            scratch_shapes=[pltpu.VMEM((2, n_slab, d, ps), F32), pltpu.VMEM((2, n_slab, d, ps), F32),
                            pltpu.SemaphoreType.DMA((2, n_slab)), pltpu.SemaphoreType.DMA((2, n_slab))],
        ),
        out_shape=jax.ShapeDtypeStruct((db, hh, d), F32),
        compiler_params=_params(("arbitrary",)),
        name="moba_sample_attn",
    )(pages_flat, blocks_flat, slopes, q3, kn3, vn3, pool_kt, pool_vt)


def _moba_sample(q, k_new, v_new, pool_k, pool_v, page_table, slopes):
    db, n_pages = page_table.shape
    _, ps, hh, d = pool_k.shape
    assert MOBA_BLOCK % ps == 0 and hh == MOBA_HEADS and d == MOBA_HEAD_DIM
    ppb = MOBA_BLOCK // ps
    past = n_pages * ps
    assert past % MOBA_BLOCK == 0 and past // MOBA_BLOCK >= MOBA_TOPK
    pool_kt = jnp.transpose(pool_k, (0, 2, 3, 1))
    pool_vt = jnp.transpose(pool_v, (0, 2, 3, 1))
    q3 = q.reshape(db, hh, d)
    blocks = _moba_route(q3, pool_kt, page_table.reshape(-1), n_pages, ppb)[:, :, :MOBA_TOPK]
    pidx = (blocks[..., None] * ppb + jnp.arange(ppb, dtype=jnp.int32)).reshape(db, hh * MOBA_TOPK * ppb)
    pages = jnp.take_along_axis(page_table, pidx, axis=1)
    o = _moba_sample_attn(q3, k_new.reshape(db, hh, d), v_new.reshape(db, hh, d), pool_kt, pool_vt,
                          pages.reshape(-1), blocks.reshape(-1), slopes, MOBA_TOPK * ppb, ppb, past)
    return o.reshape(db, hh * d)


def _mid_kernel(om_ref, og_ref, x_ref, wo_ref, nw_ref, wq_ref, h_ref, q_ref):
    half = om_ref.shape[1]
    h = x_ref[...] + (_dot(om_ref[...].astype(BF16), wo_ref[0:half, :])
                      + _dot(og_ref[...].astype(BF16), wo_ref[half:, :]))
    h_ref[...] = h
    q_ref[...] = _dot(_rms(h, nw_ref[...]).astype(BF16), wq_ref[...])


def _mid(o_moba, o_gdn, x, w_out, norm_w, w_xq, tm):
    m, dm = x.shape
    half = o_moba.shape[1]
    assert m % tm == 0
    tile = lambda wd: pl.BlockSpec((tm, wd), lambda i: (i, 0))
    return pl.pallas_call(
        _mid_kernel,
        grid=(m // tm,),
        in_specs=[tile(half), tile(o_gdn.shape[1]), tile(dm), _const_spec(w_out.shape), _const_spec((1, dm)),
                  _const_spec(w_xq.shape)],
        out_specs=[tile(dm), tile(w_xq.shape[1])],
        out_shape=[jax.ShapeDtypeStruct((m, dm), F32), jax.ShapeDtypeStruct((m, w_xq.shape[1]), F32)],
        compiler_params=_params(("parallel",)),
        name="out_proj_xq",
    )(o_moba, o_gdn, x, w_out, norm_w.reshape(1, dm), w_xq)


def _xattn_prompt_kernel(q_ref, mk_ref, mv_ref, o_ref, *, hd):
    for h in range(X_HEADS):
        hs = slice(h * hd, (h + 1) * hd)
        qh = (q_ref[0, :, hs] * (hd ** -0.5)).astype(BF16)
        s = _dot_nt(qh, mk_ref[0, :, hs].astype(BF16))
        m = jnp.max(s, axis=1, keepdims=True)
        p = jnp.exp(s - m)
        l = jnp.sum(p, axis=1, keepdims=True)
        o_ref[0, :, hs] = _dot(p.astype(BF16), mv_ref[0, :, hs].astype(BF16)) / l


def _xattn_prompt(qx, mk, mv, tm):
    b, t, w = qx.shape
    nm = mk.shape[1]
    assert t % tm == 0
    q_spec = pl.BlockSpec((1, tm, w), lambda bi, ti: (bi, ti, 0))
    m_spec = pl.BlockSpec((1, nm, w), lambda bi, ti: (bi, 0, 0))
    return pl.pallas_call(
        functools.partial(_xattn_prompt_kernel, hd=w // X_HEADS),
        grid=(b, t // tm),
        in_specs=[q_spec, m_spec, m_spec],
        out_specs=q_spec,
        out_shape=jax.ShapeDtypeStruct((b, t, w), F32),
        compiler_params=_params(("parallel", "parallel")),
        name="xattn_prompt",
    )(qx, mk, mv)


def _halving_reduce(x, op):
    n = x.shape[0]
    assert n & (n - 1) == 0
    while n > 1:
        n //= 2
        x = op(x[:n], x[n:])
    return x


def _xattn_sample_kernel(q_ref, mk_ref, mv_ref, o_ref, *, hd):
    q = q_ref[0] * (hd ** -0.5)
    s = jnp.sum(mk_ref[0] * q[None], axis=-1, keepdims=True)
    m = _halving_reduce(s, jnp.maximum)
    p = jnp.exp(s - m)
    l = _halving_reduce(p, jnp.add)
    o_ref[0] = _halving_reduce(p * mv_ref[0], jnp.add)[0] / l[0]


def _xattn_sample(qx, mk, mv):
    db, nm, hh, hd = mk.shape
    q_spec = pl.BlockSpec((1, hh, hd), lambda bi: (bi, 0, 0))
    m_spec = pl.BlockSpec((1, nm, hh, hd), lambda bi: (bi, 0, 0, 0))
    return pl.pallas_call(
        functools.partial(_xattn_sample_kernel, hd=hd),
        grid=(db,),
        in_specs=[q_spec, m_spec, m_spec],
        out_specs=q_spec,
        out_shape=jax.ShapeDtypeStruct((db, hh, hd), F32),
        compiler_params=_params(("parallel",)),
        name="xattn_sample",
    )(qx.reshape(db, hh, hd), mk, mv).reshape(db, hh * hd)


def _tail_kernel(h_ref, ox_ref, wxo_ref, nf_ref, wgu_ref, wd_ref, fn_ref, y_ref, *, d_ff, chunk):
    h = h_ref[...] + _dot(ox_ref[...].astype(BF16), wxo_ref[...])
    hn = _rms(h, nf_ref[...]).astype(BF16)
    acc = jnp.zeros(h.shape, F32)
    for c0 in range(0, d_ff, chunk):
        g = _dot(hn, wgu_ref[:, c0:c0 + chunk])
        u = _dot(hn, wgu_ref[:, d_ff + c0:d_ff + c0 + chunk])
        acc = acc + _dot((_silu(g) * u).astype(BF16), wd_ref[c0:c0 + chunk, :])
    y_ref[...] = _rms(h + acc, fn_ref[...])


def _tail(h, ox, w_xo, norm_ffn_w, w_gu, w_down, final_norm_w, tm):
    m, dm = h.shape
    d_ff = w_down.shape[0]
    chunk = 2 * LANES
    assert m % tm == 0 and d_ff % chunk == 0
    tile = lambda wd: pl.BlockSpec((tm, wd), lambda i: (i, 0))
    return pl.pallas_call(
        functools.partial(_tail_kernel, d_ff=d_ff, chunk=chunk),
        grid=(m // tm,),
        in_specs=[tile(dm), tile(ox.shape[1]), _const_spec(w_xo.shape), _const_spec((1, dm)),
                  _const_spec(w_gu.shape), _const_spec(w_down.shape), _const_spec((1, dm))],
        out_specs=tile(dm),
        out_shape=jax.ShapeDtypeStruct((m, dm), F32),
        compiler_params=_params(("parallel",)),
        name="xo_swiglu_norm",
    )(h, ox, w_xo, norm_ffn_w.reshape(1, dm), w_gu, w_down, final_norm_w.reshape(1, dm))


def _row_tile(m, pref):
    return pref if m % pref == 0 else m


def kernel(x_prompt, x_sample, cache_k, cache_v, page_table, state_conv, state_gdn, cache_mem_k, cache_mem_v, mem_prompt, norm_mix_w, w_in, conv_w, a_log, dt_bias, gdn_norm_w, w_out, norm_x_w, mem_norm_w, w_xq, w_xkv, w_xo, norm_ffn_w, w_gu, w_down, final_norm_w):
    assert w_in.shape[0] == 1, "one layer"
    b, t, dm = x_prompt.shape
    db, ds, _ = x_sample.shape
    assert ds == 1
    mw, gw = MOBA_WIDTH, GDN_WIDTH
    n_in = 3 * mw + 4 * gw + 2 * GDN_HEADS
    assert w_in.shape[2] == n_in

    w_in_p = jnp.pad(w_in[0], ((0, 0), (0, 3 * mw + 4 * gw + LANES - n_in)))
    w_in_hi = w_in_p.astype(BF16)
    w_in_lo = (w_in_p[:, :2 * mw] - w_in_hi[:, :2 * mw].astype(F32)).astype(BF16)
    in_splits = (mw, mw, mw, 3 * gw, gw, LANES)
    w_out_b, w_xq_b, w_xo_b = w_out[0].astype(BF16), w_xq[0].astype(BF16), w_xo[0].astype(BF16)
    w_gu_b, w_down_b, w_xkv_b = w_gu[0].astype(BF16), w_down[0].astype(BF16), w_xkv[0].astype(BF16)
    xw = w_xq.shape[2]
    pad_h = lambda v: jnp.pad(v.astype(F32), (0, LANES - GDN_HEADS)).reshape(1, LANES)
    alog_pad, dtb_pad = pad_h(a_log[0]), pad_h(dt_bias[0])
    slopes = jnp.exp2(-8.0 * jnp.arange(1, MOBA_HEADS + 1, dtype=F32) / MOBA_HEADS)

    mp = b * t
    xp = x_prompt.reshape(mp, dm)
    qm, kt, vt, gqkv, z, ab = _norm_matmul(xp, norm_mix_w[0], w_in_hi, w_in_lo, in_splits, _row_tile(t, 512),
                                           head_major=(1, 2), seq_len=t, head_dim=MOBA_HEAD_DIM)
    mk_p, mv_p = _norm_matmul(mem_prompt.reshape(-1, dm), mem_norm_w[0], w_xkv_b, None, (xw, xw),
                              _row_tile(mem_prompt.shape[0] * mem_prompt.shape[1], 512))
    n_mem = mem_prompt.shape[1]
    o_moba = _moba_prompt(qm.reshape(b, t, mw), kt, vt, slopes)
    o_gdn, s_p, cb_p = _gdn_prompt(gqkv.reshape(b, t, 3 * gw), z.reshape(b, t, gw), ab.reshape(b, t, LANES),
                                   conv_w[0], alog_pad, dtb_pad, gdn_norm_w[0], _row_tile(t, 512))
    h1, qx = _mid(o_moba.reshape(mp, mw), o_gdn.reshape(mp, gw), xp, w_out_b, norm_x_w[0], w_xq_b, _row_tile(mp, 512))
    ox = _xattn_prompt(qx.reshape(b, t, xw), mk_p.reshape(b, n_mem, xw), mv_p.reshape(b, n_mem, xw), _row_tile(t, 512))
    y_p = _tail(h1, ox.reshape(mp, xw), w_xo_b, norm_ffn_w[0], w_gu_b, w_down_b, final_norm_w, _row_tile(mp, 512))

    xs = x_sample.reshape(db, dm)
    qs, ks, vs, gqkv_s, z_s, ab_s = _norm_matmul(xs, norm_mix_w[0], w_in_hi, w_in_lo, in_splits, db)
    o_moba_s = _moba_sample(qs, ks, vs, cache_k[0], cache_v[0], page_table, slopes)
    o_gdn_s, s_s, cb_s = _gdn_sample(gqkv_s, z_s, ab_s, state_conv[0], state_gdn[0], conv_w[0], alog_pad, dtb_pad,
                                     gdn_norm_w[0])
    h1_s, qx_s = _mid(o_moba_s, o_gdn_s, xs, w_out_b, norm_x_w[0], w_xq_b, db)
    ox_s = _xattn_sample(qx_s, cache_mem_k[0], cache_mem_v[0])
    y_s = _tail(h1_s, ox_s, w_xo_b, norm_ffn_w[0], w_gu_b, w_down_b, final_norm_w, db)

    hh, hd = MOBA_HEADS, MOBA_HEAD_DIM
    xh = X_HEADS
    return (y_p.reshape(b, t, dm), y_s.reshape(db, 1, dm),
            jnp.transpose(kt, (0, 3, 1, 2))[None], jnp.transpose(vt, (0, 3, 1, 2))[None],
            cb_p[None], s_p[None],
            mk_p.reshape(1, b, n_mem, xh, xw // xh), mv_p.reshape(1, b, n_mem, xh, xw // xh),
            ks.reshape(1, db, 1, hh, hd), vs.reshape(1, db, 1, hh, hd),
            cb_s[None], s_s[None])
```

```python
import functools
import math

import jax
import jax.numpy as jnp
from jax import lax
from jax.experimental import pallas as pl
from jax.experimental.pallas import tpu as pltpu

F32 = jnp.float32
BF16 = jnp.bfloat16

MOBA_HEADS = 8
MOBA_HEAD_DIM = 64
MOBA_WIDTH = MOBA_HEADS * MOBA_HEAD_DIM
MOBA_BLOCK = 256
MOBA_TOPK = 3
GDN_HEAD_DIM = 128
GDN_HEADS = 4
GDN_WIDTH = GDN_HEADS * GDN_HEAD_DIM
GDN_CONV = 4
GDN_CHUNK = 64
GDN_GROUP = 8
X_HEADS = 4
RMS_EPS = 1e-6
L2_EPS = 1e-6

LANES = 128
SUBLANES = 8
VMEM_LIMIT = 56 * 1024 * 1024
NEG = -1e30


def _params(sem):
    return pltpu.CompilerParams(dimension_semantics=sem, vmem_limit_bytes=VMEM_LIMIT)


def _const_spec(shape):
    nd = len(shape)
    return pl.BlockSpec(shape, lambda *_: (0,) * nd, pipeline_mode=pl.Buffered(1))


def _dot(a, b):
    return jnp.dot(a, b, preferred_element_type=F32)


def _dot_nt(a, b):
    return lax.dot_general(a, b, (((1,), (1,)), ((), ())), preferred_element_type=F32)


def _dot_tn(a, b):
    return lax.dot_general(a, b, (((0,), (0,)), ((), ())), preferred_element_type=F32)


def _split2(x):
    hi = x.astype(BF16)
    lo = (x - hi.astype(F32)).astype(BF16)
    return hi, lo


def _split3(x):
    hi = x.astype(BF16)
    r = x - hi.astype(F32)
    mid = r.astype(BF16)
    lo = (r - mid.astype(F32)).astype(BF16)
    return hi, mid, lo


def _dot3(a, b, dot=_dot):
    ah, al = _split2(a)
    bh, bl = _split2(b)
    return dot(ah, bh) + (dot(ah, bl) + dot(al, bh))


def _dot_exact_lhs(a_bf16, b, dot=_dot):
    bh, bm, bl = _split3(b)
    return dot(a_bf16, bh) + (dot(a_bf16, bm) + dot(a_bf16, bl))


def _rms(x, w):
    return x * lax.rsqrt(jnp.mean(x * x, axis=-1, keepdims=True) + RMS_EPS) * w


def _sigmoid(x):
    return 1.0 / (1.0 + jnp.exp(-x))


def _silu(x):
    return x * _sigmoid(x)


def _softplus(x):
    return jnp.maximum(x, 0.0) + jnp.log1p(jnp.exp(-jnp.abs(x)))


def _norm_matmul_kernel(*refs, splits, n_hi, chunk, head_major, mean_rows):
    x_ref, nw_ref, w_ref = refs[:3]
    if n_hi:
        wlo_ref, out_refs = refs[3], refs[4:]
    else:
        out_refs = refs[3:]
    xn = _rms(x_ref[...], nw_ref[...])
    if mean_rows:
        out_refs, xm_ref = out_refs[:-1], out_refs[-1]
        for g in range(xn.shape[0] // mean_rows):
            xm_ref[0, g:g + 1, :] = jnp.mean(xn[g * mean_rows:(g + 1) * mean_rows, :], axis=0, keepdims=True)
    xh = xn.astype(BF16)
    if n_hi:
        xl = (xn - xh.astype(F32)).astype(BF16)
    c0 = 0
    for oi, (o_ref, width) in enumerate(zip(out_refs, splits)):
        for s in range(0, width, chunk):
            wd = min(chunk, width - s)
            a, b = c0 + s, c0 + s + wd
            acc = _dot(xh, w_ref[:, a:b])
            if b <= n_hi:
                acc = acc + (_dot(xl, w_ref[:, a:b]) + _dot(xh, wlo_ref[:, a:b]))
            if oi in head_major:
                o_ref[0] = acc.T.reshape(o_ref.shape[1:])
            else:
                o_ref[:, s:s + wd] = acc
        c0 += width


def _norm_matmul(x, norm_w, w_hi, w_lo, splits, tm, head_major=(), seq_len=None, head_dim=None, mean_rows=0):
    m, k = x.shape
    n_hi = 0 if w_lo is None else w_lo.shape[1]
    assert m % tm == 0 and sum(splits) == w_hi.shape[1]
    in_specs = [pl.BlockSpec((tm, k), lambda i: (i, 0)), _const_spec((1, k)), _const_spec(w_hi.shape)]
    args = [x, norm_w.reshape(1, k), w_hi]
    if n_hi:
        in_specs.append(_const_spec(w_lo.shape))
        args.append(w_lo)
    chunk = 512
    out_specs, out_shape = [], []
    for oi, wd in enumerate(splits):
        if oi in head_major:
            assert wd <= chunk and wd % head_dim == 0 and seq_len % tm == 0 and m % seq_len == 0
            tiles = seq_len // tm
            out_specs.append(pl.BlockSpec((1, wd // head_dim, head_dim, tm), lambda i: (i // tiles, 0, 0, i % tiles)))
            out_shape.append(jax.ShapeDtypeStruct((m // seq_len, wd // head_dim, head_dim, seq_len), F32))
        else:
            out_specs.append(pl.BlockSpec((tm, wd), lambda i: (i, 0)))
            out_shape.append(jax.ShapeDtypeStruct((m, wd), F32))
    if mean_rows:
        assert tm % mean_rows == 0
        out_specs.append(pl.BlockSpec((1, tm // mean_rows, k), lambda i: (i, 0, 0)))
        out_shape.append(jax.ShapeDtypeStruct((m // tm, tm // mean_rows, k), F32))
    outs = pl.pallas_call(
        functools.partial(_norm_matmul_kernel, splits=tuple(splits), n_hi=n_hi, chunk=chunk,
                          head_major=tuple(head_major), mean_rows=mean_rows),
        grid=(m // tm,),
        in_specs=in_specs,
        out_specs=out_specs,
        out_shape=out_shape,
        compiler_params=_params(("parallel",)),
        name="norm_matmul",
    )(*args)
    if mean_rows:
        outs = list(outs[:-1]) + [outs[-1].reshape(m // mean_rows, k)]
    return outs


def _topk_select(gates, valid, colf, ncol, axis):
    gs = [jnp.where(valid, gate, -jnp.inf) for gate in gates]
    sels = [None] * len(gs)
    for _ in range(MOBA_TOPK):
        ms = [jnp.max(g, axis=axis, keepdims=True) for g in gs]
        idxs = [jnp.min(jnp.where(g == m, colf, float(ncol)), axis=axis, keepdims=True) for g, m in zip(gs, ms)]
        picks = [colf == idx for idx in idxs]
        sels = [pick if sel is None else jnp.logical_or(sel, pick) for sel, pick in zip(sels, picks)]
        gs = [jnp.where(pick, -jnp.inf, g) for g, pick in zip(gs, picks)]
    return [jnp.logical_and(sel, valid) for sel in sels]


AUG_ALIBI = 9
AUG_SEL0 = 16
V_ROWS = LANES + 16
LOG2E = 1.4426950408889634
MOBA_UNROLL = 4


def _moba_prompt_kernel(coef_ref, q_ref, k_ref, v_ref, xm_ref, wk_ref, o_ref, kaug_ref, vt_ref, kmp_ref, qat_ref, acc_ref,
                        st_ref, p_ref, al_ref, *, blk, nb):
    hp = pl.program_id(1)
    i = pl.program_id(2)
    d = MOBA_HEAD_DIM
    i_f = jnp.asarray(i, F32)

    @pl.when(i == 0)
    def _():
        kmp_ref[0:nb, :] = _dot3(xm_ref[0], wk_ref[...])
        if kmp_ref.shape[0] > nb:
            kmp_ref[nb:, :] = jnp.zeros((kmp_ref.shape[0] - nb, LANES), F32)
        lane = lax.broadcasted_iota(jnp.int32, (blk, LANES), 1)
        rowf = lax.broadcasted_iota(jnp.int32, (blk, LANES), 0).astype(F32)

        for j in range(nb):
            cols = slice(j * blk, (j + 1) * blk)
            kj = k_ref[0, :, :, cols].reshape(LANES, blk).T
            aug = jnp.where(lane < 3, float(j), jnp.where(lane < 6, 1.0, jnp.where(lane < AUG_ALIBI, rowf, 0.0)))
            aug = jnp.where(lane == AUG_SEL0 + j, 1.0, aug)
            kaug_ref[j, :, 0:LANES] = kj.astype(BF16)
            kaug_ref[j, :, LANES:2 * LANES] = aug.astype(BF16)
            vt_ref[j, 0:LANES, :] = v_ref[0, :, :, cols].reshape(LANES, blk).astype(BF16)
            vt_ref[j, LANES:V_ROWS, :] = jnp.ones((V_ROWS - LANES, blk), BF16)

    qt = q_ref[0].T
    row = lax.broadcasted_iota(jnp.int32, (LANES, blk), 0)
    nbr = kmp_ref.shape[0]
    brow = lax.broadcasted_iota(jnp.int32, (nbr, blk), 0)
    browf = brow.astype(F32)
    crow = lax.broadcasted_iota(jnp.int32, (AUG_SEL0, blk), 0)
    kmp = kmp_ref[...]
    qhs = [jnp.where((row >= d * h2) & (row < d * (h2 + 1)), qt, 0.0) for h2 in range(2)]
    gates = [_dot3(kmp, qh) for qh in qhs]
    sels = _topk_select(gates, brow < i, browf, nbr, 0)
    for h2 in range(2):
        head = 2 * hp + h2
        selbias = jnp.where(jnp.logical_or(sels[h2], brow == i), 0.0, NEG)
        coef = jnp.zeros((AUG_SEL0, blk), F32)
        c_blk = jnp.full((1, blk), -(coef_ref[head, 6] * i_f), F32)
        for li, part in zip((3, 4, 5), _split3(c_blk)):
            coef = jnp.where(crow == li, part.astype(F32), coef)
        for li, ci in ((0, 0), (1, 1), (2, 2), (6, 3), (7, 4), (8, 5)):
            coef = jnp.where(crow == li, coef_ref[head, ci], coef)
        qat_ref[h2, 0:LANES, :] = (qhs[h2] * (d ** -0.5 * LOG2E)).astype(BF16)
        qat_ref[h2, LANES:LANES + AUG_SEL0, :] = coef.astype(BF16)
        qat_ref[h2, LANES + AUG_SEL0:LANES + AUG_SEL0 + nbr, :] = selbias.astype(BF16)
        qat_ref[h2, LANES + AUG_SEL0 + nbr:, :] = jnp.zeros((LANES - AUG_SEL0 - nbr, blk), BF16)

    def stage_a(slot, kb):
        for h2 in range(2):
            st_ref[slot, h2] = _dot(kaug_ref[kb], qat_ref[h2])

    def stage_b(slot, ms):
        new = []
        for h2 in range(2):
            st = st_ref[slot, h2]
            m_new = jnp.maximum(ms[h2], jnp.max(st, axis=0, keepdims=True))
            al_ref[slot, h2] = jnp.exp2(ms[h2] - m_new)
            p_ref[slot, h2] = jnp.exp2(st - m_new).astype(BF16)
            new.append(m_new)
        return tuple(new)

    def stage_c(slot, kb):
        for h2 in range(2):
            acc_ref[h2] = al_ref[slot, h2] * acc_ref[h2] + _dot(vt_ref[kb], p_ref[slot, h2])

    acc_ref[...] = jnp.zeros_like(acc_ref)
    p_ref[1] = jnp.zeros(p_ref.shape[1:], BF16)
    al_ref[1] = jnp.ones(al_ref.shape[1:], F32)
    keyi = lax.broadcasted_iota(jnp.int32, (blk, blk), 0)
    qryi = lax.broadcasted_iota(jnp.int32, (blk, blk), 1)
    for h2 in range(2):
        st_ref[0, h2] = jnp.where(keyi <= qryi, _dot(kaug_ref[i], qat_ref[h2]), NEG)
    ms = (jnp.full((1, blk), NEG, F32),) * 2

    def step(s, slot, ms):
        stage_c(slot, jnp.where(s == 2, i, jnp.maximum(s - 3, 0)))
        ms = stage_b(1 - slot, ms)
        stage_a(slot, s - 1)
        return ms

    def unrolled(t, ms):
        for u in range(MOBA_UNROLL):
            ms = step(MOBA_UNROLL * t + 1 + u, (1 + u) % 2, ms)
        return ms

    def pair(t, ms):
        return step(t + 1, 0, step(t, 1, ms))

    n_full = i // MOBA_UNROLL
    ms = lax.fori_loop(0, n_full, unrolled, ms)
    done = MOBA_UNROLL * n_full + 1
    n_pair = (i + 1 - done) // 2
    ms = lax.fori_loop(0, n_pair, lambda t, ms: pair(done + 2 * t, ms), ms)
    ms = lax.fori_loop(done + 2 * n_pair, i + 1, lambda s, ms: step(s, s & 1, ms), ms)
    last = i & 1
    stage_c(1 - last, jnp.where(i == 1, i, jnp.maximum(i - 2, 0)))
    stage_b(last, ms)
    stage_c(last, jnp.maximum(i - 1, 0))

    outs = []
    for h2 in range(2):
        acc = acc_ref[h2]
        outs.append(acc[0:LANES, :] / acc[LANES:LANES + 1, :])
    o_ref[0] = jnp.where(row < d, outs[0], outs[1]).T


def _moba_prompt(q, kt, vt, xmean, w_in, k_col0, slopes):
    b, t, w = q.shape
    blk = MOBA_BLOCK
    assert t % blk == 0 and w == MOBA_WIDTH and blk == 2 * LANES
    assert kt.shape == (b, MOBA_HEADS, MOBA_HEAD_DIM, t) and 2 * MOBA_HEAD_DIM == LANES
    nb = t // blk
    nbr = -(-nb // 16) * 16
    assert AUG_SEL0 + nbr < LANES and nb <= 256 and AUG_ALIBI <= AUG_SEL0 and AUG_SEL0 % 16 == 0
    assert xmean.shape[:2] == (b, nb) and k_col0 % LANES == 0
    c1 = slopes * LOG2E
    c256 = c1 * blk
    coef = jnp.stack([p.astype(F32) for p in _split3(c256)] + [p.astype(F32) for p in _split3(c1)]
                     + [c256, jnp.zeros_like(c1)], axis=1)
    kv_spec = pl.BlockSpec((1, 2, MOBA_HEAD_DIM, t), lambda bi, hp, i: (bi, hp, 0, 0))
    q_spec = pl.BlockSpec((1, blk, LANES), lambda bi, hp, i: (bi, i, hp))
    return pl.pallas_call(
        functools.partial(_moba_prompt_kernel, blk=blk, nb=nb),
        grid=(b, w // LANES, nb),
        in_specs=[pl.BlockSpec(memory_space=pltpu.SMEM), q_spec, kv_spec, kv_spec,
                  pl.BlockSpec((1, nb, xmean.shape[2]), lambda bi, hp, i: (bi, 0, 0)),
                  pl.BlockSpec((w_in.shape[0], LANES), lambda bi, hp, i: (0, k_col0 // LANES + hp))],
        out_specs=q_spec,
        out_shape=jax.ShapeDtypeStruct((b, t, w), F32),
        scratch_shapes=[pltpu.VMEM((nb, blk, 2 * LANES), BF16),
                        pltpu.VMEM((nb, V_ROWS, blk), BF16),
                        pltpu.VMEM((nbr, LANES), F32),
                        pltpu.VMEM((2, 2 * LANES, blk), BF16),
                        pltpu.VMEM((2, V_ROWS, blk), F32),
                        pltpu.VMEM((2, 2, blk, blk), F32),
                        pltpu.VMEM((2, 2, blk, blk), BF16),
                        pltpu.VMEM((2, 2, 1, blk), F32)],
        compiler_params=_params(("parallel", "parallel", "arbitrary")),
        name="moba_prompt",
    )(coef, q, kt, vt, xmean, w_in)


def _tri_inverses(lows, c):
    row = lax.broadcasted_iota(jnp.int32, (c, c), 0)
    col = lax.broadcasted_iota(jnp.int32, (c, c), 1)
    eye = jnp.where(row == col, 1.0, 0.0)
    ts = [eye - low for low in lows]
    ps = [low.astype(BF16) for low in lows]
    for _ in range(int(math.log2(c)) - 1):
        ps = [_dot(p, p).astype(BF16) for p in ps]
        ts = [t + _dot(t.astype(BF16), p) for t, p in zip(ts, ps)]
    return ts


def _gdn_prompt_kernel(x_ref, z_ref, ab_ref, cw_ref, alog_ref, dtb_ref, nw_ref,
                       o_ref, s_out_ref, cb_out_ref,
                       xext_ref, s_ref, qn_ref, kn_ref, vv_ref, gc_ref, beta_ref,
                       el_ref, m_ref, n_ref, qe_ref, o0_ref, *, tt, c):
    t_idx = pl.program_id(1)
    nt = pl.num_programs(1)
    hd = GDN_HEAD_DIM
    gw = GDN_WIDTH
    halo = SUBLANES

    @pl.when(t_idx == 0)
    def _():
        s_ref[...] = jnp.zeros_like(s_ref)
        xext_ref[0:halo, :] = jnp.zeros((halo, 3 * gw), F32)

    xext_ref[halo:halo + tt, :] = x_ref[0]
    conv = xext_ref[halo:halo + tt, :] * cw_ref[GDN_CONV - 1:GDN_CONV, :]
    for j in range(GDN_CONV - 1):
        off = halo - (GDN_CONV - 1) + j
        conv = conv + xext_ref[off:off + tt, :] * cw_ref[j:j + 1, :]

    @pl.when(t_idx == nt - 1)
    def _():
        cb_out_ref[0] = xext_ref[halo + tt - (GDN_CONV - 1):halo + tt, :]

    xext_ref[0:halo, :] = xext_ref[tt:tt + halo, :]

    act = _silu(conv)
    for h in range(GDN_HEADS):
        qh = act[:, h * hd:(h + 1) * hd]
        kh = act[:, gw + h * hd:gw + (h + 1) * hd]
        qn_ref[:, h * hd:(h + 1) * hd] = qh * lax.rsqrt(jnp.sum(qh * qh, axis=-1, keepdims=True) + L2_EPS) * (hd ** -0.5)
        kn_ref[:, h * hd:(h + 1) * hd] = kh * lax.rsqrt(jnp.sum(kh * kh, axis=-1, keepdims=True) + L2_EPS)
    vv_ref[...] = act[:, 2 * gw:]

    abv = ab_ref[0]
    g_all = -jnp.exp(alog_ref[...]) * _softplus(abv + dtb_ref[...])
    beta_ref[...] = _sigmoid(abv)
    rc = lax.broadcasted_iota(jnp.int32, (c, c), 0)
    cc = lax.broadcasted_iota(jnp.int32, (c, c), 1)
    tri = jnp.where(rc >= cc, 1.0, 0.0).astype(BF16)
    for ci in range(tt // c):
        gc_ref[ci * c:(ci + 1) * c, :] = _dot_exact_lhs(tri, g_all[ci * c:(ci + 1) * c, :])

    incl = rc >= cc
    strict = rc > cc
    lane8 = lax.broadcasted_iota(jnp.int32, (SUBLANES, LANES), 1)
    nw = nw_ref[...]

    def prepare_group(gi, carry):
        probs = []
        for cj in range(GDN_GROUP):
            ci = gi * GDN_GROUP + cj
            r0 = pl.multiple_of(ci * c, c)
            gcs = gc_ref[pl.ds(r0, c), :]
            bet = beta_ref[pl.ds(r0, c), :]
            e_gc = jnp.exp(gcs)
            g_last = gcs[c - 1:c, :]
            e_rev = jnp.exp(g_last - gcs)
            el_ref[pl.ds(ci, 1), :] = jnp.exp(g_last)
            for h in range(GDN_HEADS):
                hs = slice(h * hd, (h + 1) * hd)
                onehot = jnp.where(lane8 == h, 1.0, 0.0).astype(BF16)
                grow = _dot_exact_lhs(onehot, gcs, _dot_nt)[0:1, :]
                probs.append(dict(
                    idx=ci * GDN_HEADS + h, qn=qn_ref[pl.ds(r0, c), hs], kn=kn_ref[pl.ds(r0, c), hs],
                    vv=vv_ref[pl.ds(r0, c), hs], bcol=bet[:, GDN_HEADS + h:GDN_HEADS + h + 1],
                    egc=e_gc[:, h:h + 1], erev=e_rev[:, h:h + 1],
                    decay=jnp.where(incl, jnp.exp(gcs[:, h:h + 1] - grow), 0.0)))
        for p in probs:
            p["knb"] = p["kn"].astype(BF16)
        lows = [jnp.where(strict, p["bcol"] * _dot_nt(p["knb"], p["knb"]) * p["decay"], 0.0) for p in probs]
        tinvs = _tri_inverses(lows, c)
        sols = []
        for p, tinv in zip(probs, tinvs):
            rhs = jnp.concatenate([p["kn"] * (p["bcol"] * p["egc"]), p["vv"] * p["bcol"]], axis=1)
            sols.append(_dot(tinv.astype(BF16), rhs.astype(BF16)).astype(BF16))
        for p, sol in zip(probs, sols):
            aqk = (_dot_nt(p["qn"].astype(BF16), p["knb"]) * p["decay"]).astype(BF16)
            upd = _dot_tn((p["kn"] * p["erev"]).astype(BF16), sol)
            out = _dot(aqk, sol)
            m_ref[p["idx"]] = upd[:, 0:hd].astype(BF16)
            n_ref[p["idx"]] = upd[:, hd:]
            qe_ref[p["idx"]] = (p["qn"] * p["egc"] - out[:, 0:hd]).astype(BF16)
            o0_ref[p["idx"]] = out[:, hd:]
        return carry

    lax.fori_loop(0, tt // (GDN_GROUP * c), prepare_group, 0)

    def advance(ci, carry):
        r0 = pl.multiple_of(ci * c, c)
        e_last = el_ref[pl.ds(ci, 1), :]
        for h in range(GDN_HEADS):
            hs = slice(h * hd, (h + 1) * hd)
            idx = ci * GDN_HEADS + h
            s = s_ref[h]
            sb = s.astype(BF16)
            o = _dot(qe_ref[idx], sb) + o0_ref[idx]
            s_ref[h] = s * e_last[:, h:h + 1] - _dot(m_ref[idx], sb) + n_ref[idx]
            o = o * lax.rsqrt(jnp.mean(o * o, axis=-1, keepdims=True) + RMS_EPS) * nw
            o_ref[0, pl.ds(r0, c), hs] = o * _silu(z_ref[0, pl.ds(r0, c), hs])
        return carry

    lax.fori_loop(0, tt // c, advance, 0)

    @pl.when(t_idx == nt - 1)
    def _():
        s_out_ref[0] = s_ref[...]


def _gdn_prompt(gqkv, z, ab, conv_w, alog_pad, dtb_pad, norm_w, tt):
    b, t, w3 = gqkv.shape
    c = GDN_CHUNK
    assert t % tt == 0 and tt % (GDN_GROUP * c) == 0 and w3 == 3 * GDN_WIDTH
    hd = GDN_HEAD_DIM
    nch = tt // c
    tile = lambda wd: pl.BlockSpec((1, tt, wd), lambda bi, ti: (bi, ti, 0))
    return pl.pallas_call(
        functools.partial(_gdn_prompt_kernel, tt=tt, c=c),
        grid=(b, t // tt),
        in_specs=[tile(w3), tile(GDN_WIDTH), tile(LANES), _const_spec(conv_w.shape),
                  _const_spec((1, LANES)), _const_spec((1, LANES)), _const_spec((1, hd))],
        out_specs=[tile(GDN_WIDTH),
                   pl.BlockSpec((1, GDN_HEADS, hd, hd), lambda bi, ti: (bi, 0, 0, 0)),
                   pl.BlockSpec((1, GDN_CONV - 1, w3), lambda bi, ti: (bi, 0, 0))],
        out_shape=[jax.ShapeDtypeStruct((b, t, GDN_WIDTH), F32),
                   jax.ShapeDtypeStruct((b, GDN_HEADS, hd, hd), F32),
                   jax.ShapeDtypeStruct((b, GDN_CONV - 1, w3), F32)],
        scratch_shapes=[pltpu.VMEM((tt + SUBLANES, w3), F32),
                        pltpu.VMEM((GDN_HEADS, hd, hd), F32),
                        pltpu.VMEM((tt, GDN_WIDTH), F32),
                        pltpu.VMEM((tt, GDN_WIDTH), F32),
                        pltpu.VMEM((tt, GDN_WIDTH), F32),
                        pltpu.VMEM((tt, LANES), F32),
                        pltpu.VMEM((tt, LANES), F32),
                        pltpu.VMEM((nch, LANES), F32),
                        pltpu.VMEM((nch * GDN_HEADS, hd, hd), BF16),
                        pltpu.VMEM((nch * GDN_HEADS, hd, hd), F32),
                        pltpu.VMEM((nch * GDN_HEADS, c, hd), BF16),
                        pltpu.VMEM((nch * GDN_HEADS, c, hd), F32)],
        compiler_params=_params(("parallel", "arbitrary")),
        name="gdn_prompt",
    )(gqkv, z, ab, conv_w, alog_pad, dtb_pad, norm_w.reshape(1, hd))


def _gdn_sample_kernel(x_ref, z_ref, ab_ref, cb_ref, s_in_ref, cw_ref, alog_ref, dtb_ref, nw_ref,
                       o_ref, s_out_ref, cb_out_ref):
    hd = GDN_HEAD_DIM
    gw = GDN_WIDTH
    nc = GDN_CONV - 1
    x = x_ref[0]
    cb = cb_ref[0]
    conv = x * cw_ref[nc:nc + 1, :]
    for j in range(nc):
        conv = conv + cb[j:j + 1, :] * cw_ref[j:j + 1, :]
    cb_out_ref[0, 0:nc - 1, :] = cb[1:nc, :]
    cb_out_ref[0, nc - 1:nc, :] = x
    act = _silu(conv)
    abv = ab_ref[0]
    g_all = -jnp.exp(alog_ref[...]) * _softplus(abv + dtb_ref[...])
    e_g = jnp.exp(g_all)
    beta = _sigmoid(abv)
    z = z_ref[0]
    nw = nw_ref[...]
    eye = (lax.broadcasted_iota(jnp.int32, (hd, hd), 0) == lax.broadcasted_iota(jnp.int32, (hd, hd), 1))

    def as_column(row):
        return jnp.sum(jnp.where(eye, row, 0.0), axis=1, keepdims=True)

    heads = range(GDN_HEADS)
    hss = [slice(h * hd, (h + 1) * hd) for h in heads]
    qhs = [act[:, hs] for hs in hss]
    khs = [act[:, gw + h * hd:gw + (h + 1) * hd] for h in heads]
    vvs = [act[:, 2 * gw + h * hd:2 * gw + (h + 1) * hd] for h in heads]
    qns = [qh * lax.rsqrt(jnp.sum(qh * qh, axis=-1, keepdims=True) + L2_EPS) * (hd ** -0.5) for qh in qhs]
    kns = [kh * lax.rsqrt(jnp.sum(kh * kh, axis=-1, keepdims=True) + L2_EPS) for kh in khs]
    egs = [e_g[:, h:h + 1] for h in heads]
    bhs = [beta[:, GDN_HEADS + h:GDN_HEADS + h + 1] for h in heads]
    ss = [s_in_ref[0, h] for h in heads]
    k_cols = [as_column(kn) for kn in kns]
    q_cols = [as_column(qn) for qn in qns]
    v_news = [vvs[h] * bhs[h] - jnp.sum((k_cols[h] * (bhs[h] * egs[h])) * ss[h], axis=0, keepdims=True) for h in heads]
    qks = [jnp.sum(qns[h] * kns[h], axis=-1, keepdims=True) for h in heads]
    os_ = [jnp.sum((q_cols[h] * egs[h]) * ss[h], axis=0, keepdims=True) + qks[h] * v_news[h] for h in heads]
    for h in heads:
        s_out_ref[0, h] = ss[h] * egs[h] + k_cols[h] * v_news[h]
    for h in heads:
        o = os_[h] * lax.rsqrt(jnp.mean(os_[h] * os_[h], axis=-1, keepdims=True) + RMS_EPS) * nw
        o_ref[0, :, hss[h]] = o * _silu(z[:, hss[h]])


def _gdn_sample(gqkv, z, ab, conv_buf, s0, conv_w, alog_pad, dtb_pad, norm_w):
    db, w3 = gqkv.shape
    hd = GDN_HEAD_DIM
    nc = GDN_CONV - 1
    row = lambda wd: pl.BlockSpec((1, 1, wd), lambda bi: (bi, 0, 0))
    st = pl.BlockSpec((1, GDN_HEADS, hd, hd), lambda bi: (bi, 0, 0, 0))
    cbs = pl.BlockSpec((1, nc, w3), lambda bi: (bi, 0, 0))
    o, s_new, cb_new = pl.pallas_call(
        _gdn_sample_kernel,
        grid=(db,),
        in_specs=[row(w3), row(GDN_WIDTH), row(LANES), cbs, st, _const_spec(conv_w.shape),
                  _const_spec((1, LANES)), _const_spec((1, LANES)), _const_spec((1, hd))],
        out_specs=[row(GDN_WIDTH), st, cbs],
        out_shape=[jax.ShapeDtypeStruct((db, 1, GDN_WIDTH), F32),
                   jax.ShapeDtypeStruct((db, GDN_HEADS, hd, hd), F32),
                   jax.ShapeDtypeStruct((db, nc, w3), F32)],
        compiler_params=_params(("parallel",)),
        name="gdn_sample",
    )(gqkv.reshape(db, 1, w3), z.reshape(db, 1, GDN_WIDTH), ab.reshape(db, 1, LANES), conv_buf, s0,
      conv_w, alog_pad, dtb_pad, norm_w.reshape(1, hd))
    return o.reshape(db, GDN_WIDTH), s_new, cb_new


DMA_RING = 8
ROUTE_BANKS = 4


def _moba_route_kernel(pt_ref, q_ref, pool_ref, o_ref, kbuf, sem, qc_ref, *, n_pages, ppb):
    b = pl.program_id(0)
    n_seq = pl.num_programs(0)
    n_groups = n_pages // DMA_RING
    bpg = DMA_RING // ppb
    nblk = n_pages // ppb
    hh, d = MOBA_HEADS, MOBA_HEAD_DIM

    def page_copy(seq, g, u, bank):
        page = pt_ref[seq * n_pages + g * DMA_RING + u]
        return pltpu.make_async_copy(pool_ref.at[page], kbuf.at[bank, u], sem.at[bank, u])

    def start_group(seq, g, bank):
        for u in range(DMA_RING):
            page_copy(seq, g, u, bank).start()

    ahead = ROUTE_BANKS - 1

    @pl.when(b == 0)
    def _():
        for g0 in range(ahead):
            start_group(b, g0, g0)

    q = q_ref[0]
    eye = lax.broadcasted_iota(jnp.int32, (d, d), 0) == lax.broadcasted_iota(jnp.int32, (d, d), 1)
    for h in range(hh):
        q_col = jnp.sum(jnp.where(eye, q[h:h + 1, :], 0.0), axis=1, keepdims=True)
        qc_ref[h] = jnp.broadcast_to(q_col, (d, LANES))
    lane = lax.broadcasted_iota(jnp.int32, (hh, LANES), 1)
    rowi = lax.broadcasted_iota(jnp.int32, (hh, LANES), 0)

    def group(g, gate):
        bank = g & (ROUTE_BANKS - 1)
        for u in range(DMA_RING):
            page_copy(b, g, u, bank).wait()

        free_bank = (g + ahead) & (ROUTE_BANKS - 1)

        @pl.when(g + ahead < n_groups)
        def _():
            start_group(b, g + ahead, free_bank)

        @pl.when(jnp.logical_and(g + ahead >= n_groups, b + 1 < n_seq))
        def _():
            start_group(b + 1, g + ahead - n_groups, free_bank)

        for kb in range(bpg):
            x = kbuf[bank, kb * ppb]
            for u in range(1, ppb):
                x = x + kbuf[bank, kb * ppb + u]
            w = (x * qc_ref[...]).reshape(hh, d // SUBLANES, SUBLANES, LANES)
            r = jnp.sum(w, axis=1)
            per_head = jnp.zeros((hh, LANES), F32)
            for h in range(hh):
                per_head = jnp.where(rowi == h, jnp.sum(r[h], axis=0, keepdims=True), per_head)
            total = jnp.sum(per_head, axis=1, keepdims=True)
            gate = jnp.where(lane == g * bpg + kb, total, gate)
        return gate

    gate = lax.fori_loop(0, n_groups, group, jnp.zeros((hh, LANES), F32)) * (1.0 / MOBA_BLOCK)
    lanef = lane.astype(F32)
    gate = jnp.where(lane < nblk, gate, -jnp.inf)
    out = jnp.zeros((hh, LANES), F32)
    for r in range(MOBA_TOPK):
        m = jnp.max(gate, axis=1, keepdims=True)
        idx = jnp.min(jnp.where(gate == m, lanef, float(LANES)), axis=1, keepdims=True)
        gate = jnp.where(lanef == idx, -jnp.inf, gate)
        out = jnp.where(lane == r, idx, out)
    o_ref[0] = out.astype(jnp.int32)


def _moba_route(q3, pool_kt, pt_flat, n_pages, ppb):
    db = q3.shape[0]
    _, hh, d, ps = pool_kt.shape
    assert n_pages % (ROUTE_BANKS * DMA_RING) == 0 and ROUTE_BANKS & (ROUTE_BANKS - 1) == 0
    assert DMA_RING % ppb == 0 and ps == LANES and n_pages // ppb <= LANES
    return pl.pallas_call(
        functools.partial(_moba_route_kernel, n_pages=n_pages, ppb=ppb),
        grid_spec=pltpu.PrefetchScalarGridSpec(
            num_scalar_prefetch=1,
            grid=(db,),
            in_specs=[pl.BlockSpec((1, hh, d), lambda b, pt: (b, 0, 0)), pl.BlockSpec(memory_space=pl.ANY)],
            out_specs=pl.BlockSpec((1, hh, LANES), lambda b, pt: (b, 0, 0)),
            scratch_shapes=[pltpu.VMEM((ROUTE_BANKS, DMA_RING, hh, d, ps), F32),
                            pltpu.SemaphoreType.DMA((ROUTE_BANKS, DMA_RING)),
                            pltpu.VMEM((hh, d, LANES), F32)],
        ),
        out_shape=jax.ShapeDtypeStruct((db, hh, LANES), jnp.int32),
        compiler_params=_params(("arbitrary",)),
        name="moba_sample_route",
    )(pt_flat, q3, pool_kt)


def _moba_sample_attn_kernel(pg_ref, bk_ref, slope_ref, q_ref, kn_ref, vn_ref, pk_ref, pv_ref, o_ref,
                             kbuf, vbuf, ksem, vsem, *, per_head, ppb, ps, past):
    b = pl.program_id(0)
    n_seq = pl.num_programs(0)
    hh, d = MOBA_HEADS, MOBA_HEAD_DIM
    n_slab = hh * per_head

    def copies(seq, t, bank):
        pg = pg_ref[seq * n_slab + t]
        h = t // per_head
        return (pltpu.make_async_copy(pk_ref.at[pg, h], kbuf.at[bank, t], ksem.at[bank, t]),
                pltpu.make_async_copy(pv_ref.at[pg, h], vbuf.at[bank, t], vsem.at[bank, t]))

    def start_all(seq, bank):
        for t in range(n_slab):
            for c in copies(seq, t, bank):
                c.start()

    @pl.when(b == 0)
    def _():
        start_all(b, 0)

    bank = b & 1
    for t in range(n_slab):
        for c in copies(b, t, bank):
            c.wait()

    @pl.when(b + 1 < n_seq)
    def _():
        start_all(b + 1, 1 - bank)

    q = q_ref[0] * (d ** -0.5)
    kn = kn_ref[0]
    vn = vn_ref[0]
    lanef = lax.broadcasted_iota(jnp.int32, (1, ps), 1).astype(F32)
    qhs = [jnp.broadcast_to(q[h:h + 1, :], (SUBLANES, d)).astype(BF16) for h in range(hh)]
    scores = [_dot(qhs[t // per_head], kbuf[bank, t].astype(BF16))[0:1, :] for t in range(n_slab)]
    s_alls, s_news = [], []
    for h in range(hh):
        rows = []
        for j in range(per_head):
            t = h * per_head + j
            blk = bk_ref[b * (n_slab // ppb) + t // ppb]
            pos0 = blk * (ppb * ps) + (t % ppb) * ps
            rows.append(scores[t] - slope_ref[h] * (jnp.asarray(past - pos0, F32) - lanef))
        s_alls.append(jnp.concatenate(rows, axis=1))
        s_news.append(jnp.sum(q[h:h + 1, :] * kn[h:h + 1, :], axis=1, keepdims=True))
    ms = [jnp.maximum(jnp.max(s_all, axis=1, keepdims=True), s_new) for s_all, s_new in zip(s_alls, s_news)]
    ps_ = [jnp.exp(s_all - m) for s_all, m in zip(s_alls, ms)]
    p_news = [jnp.exp(s_new - m) for s_new, m in zip(s_news, ms)]
    outs = []
    for h in range(hh):
        v_all = jnp.concatenate([vbuf[bank, h * per_head + j].astype(BF16) for j in range(per_head)], axis=1)
        p8 = jnp.broadcast_to(ps_[h], (SUBLANES, ps_[h].shape[1])).astype(BF16)
        outs.append(_dot_nt(p8, v_all)[0:1, :])
    for h in range(hh):
        l = jnp.sum(ps_[h], axis=1, keepdims=True) + p_news[h]
        o_ref[0, h:h + 1, :] = (outs[h] + p_news[h] * vn[h:h + 1, :]) / l


def _moba_sample_attn(q3, kn3, vn3, pool_kt, pool_vt, pages_flat, blocks_flat, slopes, per_head, ppb, past):
    db, hh, d = q3.shape
    _, _, _, ps = pool_kt.shape
    n_slab = hh * per_head
    row = pl.BlockSpec((1, hh, d), lambda b, pg, bk: (b, 0, 0))
    hbm = pl.BlockSpec(memory_space=pl.ANY)
    return pl.pallas_call(
        functools.partial(_moba_sample_attn_kernel, per_head=per_head, ppb=ppb, ps=ps, past=past),
        grid_spec=pltpu.PrefetchScalarGridSpec(
            num_scalar_prefetch=2,
            grid=(db,),
            in_specs=[pl.BlockSpec(memory_space=pltpu.SMEM), row, row, row, hbm, hbm],
            out_specs=row,
            scratch_shapes=[pltpu.VMEM((2, n_slab, d, ps), F32), pltpu.VMEM((2, n_slab, d, ps), F32),
                            pltpu.SemaphoreType.DMA((2, n_slab)), pltpu.SemaphoreType.DMA((2, n_slab))],
        ),
        out_shape=jax.ShapeDtypeStruct((db, hh, d), F32),
        compiler_params=_params(("arbitrary",)),
        name="moba_sample_attn",
    )(pages_flat, blocks_flat, slopes, q3, kn3, vn3, pool_kt, pool_vt)


def _moba_sample(q, k_new, v_new, pool_k, pool_v, page_table, slopes):
    db, n_pages = page_table.shape
    _, ps, hh, d = pool_k.shape
    assert MOBA_BLOCK % ps == 0 and hh == MOBA_HEADS and d == MOBA_HEAD_DIM
    ppb = MOBA_BLOCK // ps
    past = n_pages * ps
    assert past % MOBA_BLOCK == 0 and past // MOBA_BLOCK >= MOBA_TOPK
    pool_kt = jnp.transpose(pool_k, (0, 2, 3, 1))
    pool_vt = jnp.transpose(pool_v, (0, 2, 3, 1))
    q3 = q.reshape(db, hh, d)
    blocks = _moba_route(q3, pool_kt, page_table.reshape(-1), n_pages, ppb)[:, :, :MOBA_TOPK]
    pidx = (blocks[..., None] * ppb + jnp.arange(ppb, dtype=jnp.int32)).reshape(db, hh * MOBA_TOPK * ppb)
    pages = jnp.take_along_axis(page_table, pidx, axis=1)
    o = _moba_sample_attn(q3, k_new.reshape(db, hh, d), v_new.reshape(db, hh, d), pool_kt, pool_vt,
                          pages.reshape(-1), blocks.reshape(-1), slopes, MOBA_TOPK * ppb, ppb, past)
    return o.reshape(db, hh * d)


def _memory_attention(q, mk_ref, mv_ref, o_ref, hd):
    for h in range(X_HEADS):
        hs = slice(h * hd, (h + 1) * hd)
        qh = (q[:, hs] * (hd ** -0.5)).astype(BF16)
        s = _dot_nt(qh, mk_ref[0, :, hs].astype(BF16))
        m = jnp.max(s, axis=1, keepdims=True)
        p = jnp.exp(s - m)
        l = jnp.sum(p, axis=1, keepdims=True)
        o_ref[:, hs] = _dot(p.astype(BF16), mv_ref[0, :, hs].astype(BF16)) / l


def _mid_kernel(om_ref, og_ref, x_ref, wo_ref, nw_ref, wq_ref, *rest, attend):
    half = om_ref.shape[1]
    h = x_ref[...] + (_dot(om_ref[...].astype(BF16), wo_ref[0:half, :])
                      + _dot(og_ref[...].astype(BF16), wo_ref[half:, :]))
    q = _dot(_rms(h, nw_ref[...]).astype(BF16), wq_ref[...])
    if attend:
        mk_ref, mv_ref, h_ref, o_ref = rest
        _memory_attention(q, mk_ref, mv_ref, o_ref, q.shape[1] // X_HEADS)
    else:
        h_ref, q_ref = rest
        q_ref[...] = q
    h_ref[...] = h


def _mid(o_moba, o_gdn, x, w_out, norm_w, w_xq, tm, memory=None):
    m, dm = x.shape
    half = o_moba.shape[1]
    xw = w_xq.shape[1]
    assert m % tm == 0
    tile = lambda wd: pl.BlockSpec((tm, wd), lambda i: (i, 0))
    in_specs = [tile(half), tile(o_gdn.shape[1]), tile(dm), _const_spec(w_out.shape), _const_spec((1, dm)),
                _const_spec(w_xq.shape)]
    args = [o_moba, o_gdn, x, w_out, norm_w.reshape(1, dm), w_xq]
    if memory is not None:
        mk, mv = memory
        nbat, nm, _ = mk.shape
        assert m % nbat == 0 and (m // nbat) % tm == 0
        tiles = m // nbat // tm
        in_specs += [pl.BlockSpec((1, nm, xw), lambda i: (i // tiles, 0, 0))] * 2
        args += [mk, mv]
    return pl.pallas_call(
        functools.partial(_mid_kernel, attend=memory is not None),
        grid=(m // tm,),
        in_specs=in_specs,
        out_specs=[tile(dm), tile(xw)],
        out_shape=[jax.ShapeDtypeStruct((m, dm), F32), jax.ShapeDtypeStruct((m, xw), F32)],
        compiler_params=_params(("parallel",)),
        name="out_proj_xq",
    )(*args)


def _halving_reduce(x, op):
    n = x.shape[0]
    assert n & (n - 1) == 0
    while n > 1:
        n //= 2
        x = op(x[:n], x[n:])
    return x


def _xattn_sample_kernel(q_ref, mk_ref, mv_ref, o_ref, *, hd):
    q = q_ref[0] * (hd ** -0.5)
    s = jnp.sum(mk_ref[0] * q[None], axis=-1, keepdims=True)
    m = _halving_reduce(s, jnp.maximum)
    p = jnp.exp(s - m)
    l = _halving_reduce(p, jnp.add)
    o_ref[0] = _halving_reduce(p * mv_ref[0], jnp.add)[0] / l[0]


def _xattn_sample(qx, mk, mv):
    db, nm, hh, hd = mk.shape
    q_spec = pl.BlockSpec((1, hh, hd), lambda bi: (bi, 0, 0))
    m_spec = pl.BlockSpec((1, nm, hh, hd), lambda bi: (bi, 0, 0, 0))
    return pl.pallas_call(
        functools.partial(_xattn_sample_kernel, hd=hd),
        grid=(db,),
        in_specs=[q_spec, m_spec, m_spec],
        out_specs=q_spec,
        out_shape=jax.ShapeDtypeStruct((db, hh, hd), F32),
        compiler_params=_params(("parallel",)),
        name="xattn_sample",
    )(qx.reshape(db, hh, hd), mk, mv).reshape(db, hh * hd)


def _tail_kernel(h_ref, ox_ref, wxo_ref, nf_ref, wgu_ref, wd_ref, fn_ref, y_ref, *, d_ff, chunk):
    h = h_ref[...] + _dot(ox_ref[...].astype(BF16), wxo_ref[...])
    hn = _rms(h, nf_ref[...]).astype(BF16)
    acc = jnp.zeros(h.shape, F32)
    for c0 in range(0, d_ff, chunk):
        g = _dot(hn, wgu_ref[:, c0:c0 + chunk])
        u = _dot(hn, wgu_ref[:, d_ff + c0:d_ff + c0 + chunk])
        acc = acc + _dot((_silu(g) * u).astype(BF16), wd_ref[c0:c0 + chunk, :])
    y_ref[...] = _rms(h + acc, fn_ref[...])


def _tail(h, ox, w_xo, norm_ffn_w, w_gu, w_down, final_norm_w, tm):
    m, dm = h.shape
    d_ff = w_down.shape[0]
    chunk = 2 * LANES
    assert m % tm == 0 and d_ff % chunk == 0
    tile = lambda wd: pl.BlockSpec((tm, wd), lambda i: (i, 0))
    return pl.pallas_call(
        functools.partial(_tail_kernel, d_ff=d_ff, chunk=chunk),
        grid=(m // tm,),
        in_specs=[tile(dm), tile(ox.shape[1]), _const_spec(w_xo.shape), _const_spec((1, dm)),
                  _const_spec(w_gu.shape), _const_spec(w_down.shape), _const_spec((1, dm))],
        out_specs=tile(dm),
        out_shape=jax.ShapeDtypeStruct((m, dm), F32),
        compiler_params=_params(("parallel",)),
        name="xo_swiglu_norm",
    )(h, ox, w_xo, norm_ffn_w.reshape(1, dm), w_gu, w_down, final_norm_w.reshape(1, dm))


def _row_tile(m, pref):
    return pref if m % pref == 0 else m


def kernel(x_prompt, x_sample, cache_k, cache_v, page_table, state_conv, state_gdn, cache_mem_k, cache_mem_v, mem_prompt, norm_mix_w, w_in, conv_w, a_log, dt_bias, gdn_norm_w, w_out, norm_x_w, mem_norm_w, w_xq, w_xkv, w_xo, norm_ffn_w, w_gu, w_down, final_norm_w):
    assert w_in.shape[0] == 1, "one layer"
    b, t, dm = x_prompt.shape
    db, ds, _ = x_sample.shape
    assert ds == 1
    mw, gw = MOBA_WIDTH, GDN_WIDTH
    n_in = 3 * mw + 4 * gw + 2 * GDN_HEADS
    assert w_in.shape[2] == n_in

    w_in_p = jnp.pad(w_in[0], ((0, 0), (0, 3 * mw + 4 * gw + LANES - n_in)))
    w_in_hi = w_in_p.astype(BF16)
    w_in_lo = (w_in_p[:, :mw] - w_in_hi[:, :mw].astype(F32)).astype(BF16)
    in_splits = (mw, mw, mw, 3 * gw, gw, LANES)
    w_out_b, w_xq_b, w_xo_b = w_out[0].astype(BF16), w_xq[0].astype(BF16), w_xo[0].astype(BF16)
    w_gu_b, w_down_b, w_xkv_b = w_gu[0].astype(BF16), w_down[0].astype(BF16), w_xkv[0].astype(BF16)
    xw = w_xq.shape[2]
    pad_h = lambda v: jnp.pad(v.astype(F32), (0, LANES - GDN_HEADS)).reshape(1, LANES)
    alog_pad, dtb_pad = pad_h(a_log[0]), pad_h(dt_bias[0])
    slopes = jnp.exp2(-8.0 * jnp.arange(1, MOBA_HEADS + 1, dtype=F32) / MOBA_HEADS)

    mp = b * t
    xp = x_prompt.reshape(mp, dm)
    qm, kt, vt, gqkv, z, ab, xmean = _norm_matmul(xp, norm_mix_w[0], w_in_hi, w_in_lo, in_splits, _row_tile(t, 512),
                                                  head_major=(1, 2), seq_len=t, head_dim=MOBA_HEAD_DIM,
                                                  mean_rows=MOBA_BLOCK)
    mk_p, mv_p = _norm_matmul(mem_prompt.reshape(-1, dm), mem_norm_w[0], w_xkv_b, None, (xw, xw),
                              _row_tile(mem_prompt.shape[0] * mem_prompt.shape[1], 512))
    n_mem = mem_prompt.shape[1]
    o_moba = _moba_prompt(qm.reshape(b, t, mw), kt, vt, xmean.reshape(b, t // MOBA_BLOCK, dm), w_in[0], mw, slopes)
    o_gdn, s_p, cb_p = _gdn_prompt(gqkv.reshape(b, t, 3 * gw), z.reshape(b, t, gw), ab.reshape(b, t, LANES),
                                   conv_w[0], alog_pad, dtb_pad, gdn_norm_w[0], _row_tile(t, 512))
    h1, ox = _mid(o_moba.reshape(mp, mw), o_gdn.reshape(mp, gw), xp, w_out_b, norm_x_w[0], w_xq_b, _row_tile(t, 512),
                  memory=(mk_p.reshape(b, n_mem, xw), mv_p.reshape(b, n_mem, xw)))
    y_p = _tail(h1, ox, w_xo_b, norm_ffn_w[0], w_gu_b, w_down_b, final_norm_w, _row_tile(mp, 512))

    xs = x_sample.reshape(db, dm)
    qs, ks, vs, gqkv_s, z_s, ab_s = _norm_matmul(xs, norm_mix_w[0], w_in_hi, w_in_lo, in_splits, db)
    o_moba_s = _moba_sample(qs, ks, vs, cache_k[0], cache_v[0], page_table, slopes)
    o_gdn_s, s_s, cb_s = _gdn_sample(gqkv_s, z_s, ab_s, state_conv[0], state_gdn[0], conv_w[0], alog_pad, dtb_pad,
                                     gdn_norm_w[0])
    h1_s, qx_s = _mid(o_moba_s, o_gdn_s, xs, w_out_b, norm_x_w[0], w_xq_b, db)
    ox_s = _xattn_sample(qx_s, cache_mem_k[0], cache_mem_v[0])
    y_s = _tail(h1_s, ox_s, w_xo_b, norm_ffn_w[0], w_gu_b, w_down_b, final_norm_w, db)

    hh, hd = MOBA_HEADS, MOBA_HEAD_DIM
    xh = X_HEADS
    return (y_p.reshape(b, t, dm), y_s.reshape(db, 1, dm),
            jnp.transpose(kt, (0, 3, 1, 2))[None], jnp.transpose(vt, (0, 3, 1, 2))[None],
            cb_p[None], s_p[None],
            mk_p.reshape(1, b, n_mem, xh, xw // xh), mv_p.reshape(1, b, n_mem, xh, xw // xh),
            ks.reshape(1, db, 1, hh, hd), vs.reshape(1, db, 1, hh, hd),
            cb_s[None], s_s[None])
```

```python
import functools
import math

import jax
import jax.numpy as jnp
from jax import lax
from jax.experimental import pallas as pl
from jax.experimental.pallas import tpu as pltpu

F32 = jnp.float32
BF16 = jnp.bfloat16

MOBA_HEADS = 8
MOBA_HEAD_DIM = 64
MOBA_WIDTH = MOBA_HEADS * MOBA_HEAD_DIM
MOBA_BLOCK = 256
MOBA_TOPK = 3
GDN_HEAD_DIM = 128
GDN_HEADS = 4
GDN_WIDTH = GDN_HEADS * GDN_HEAD_DIM
GDN_CONV = 4
GDN_CHUNK = 64
GDN_GROUP = 8
X_HEADS = 4
RMS_EPS = 1e-6
L2_EPS = 1e-6

LANES = 128
SUBLANES = 8
VMEM_LIMIT = 56 * 1024 * 1024
NEG = -1e30


def _params(sem):
    return pltpu.CompilerParams(dimension_semantics=sem, vmem_limit_bytes=VMEM_LIMIT)


def _const_spec(shape):
    nd = len(shape)
    return pl.BlockSpec(shape, lambda *_: (0,) * nd, pipeline_mode=pl.Buffered(1))


def _dot(a, b):
    return jnp.dot(a, b, preferred_element_type=F32)


def _dot_nt(a, b):
    return lax.dot_general(a, b, (((1,), (1,)), ((), ())), preferred_element_type=F32)


def _dot_tn(a, b):
    return lax.dot_general(a, b, (((0,), (0,)), ((), ())), preferred_element_type=F32)


def _split2(x):
    hi = x.astype(BF16)
    lo = (x - hi.astype(F32)).astype(BF16)
    return hi, lo


def _split3(x):
    hi = x.astype(BF16)
    r = x - hi.astype(F32)
    mid = r.astype(BF16)
    lo = (r - mid.astype(F32)).astype(BF16)
    return hi, mid, lo


def _dot3(a, b, dot=_dot):
    ah, al = _split2(a)
    bh, bl = _split2(b)
    return dot(ah, bh) + (dot(ah, bl) + dot(al, bh))


def _dot_exact_lhs(a_bf16, b, dot=_dot):
    bh, bm, bl = _split3(b)
    return dot(a_bf16, bh) + (dot(a_bf16, bm) + dot(a_bf16, bl))


def _rms(x, w):
    return x * lax.rsqrt(jnp.mean(x * x, axis=-1, keepdims=True) + RMS_EPS) * w


def _sigmoid(x):
    return 1.0 / (1.0 + jnp.exp(-x))


def _silu(x):
    return x * _sigmoid(x)


def _softplus(x):
    return jnp.maximum(x, 0.0) + jnp.log1p(jnp.exp(-jnp.abs(x)))


def _norm_matmul_kernel(*refs, splits, n_hi, chunk, head_major, mean_rows):
    x_ref, nw_ref, w_ref = refs[:3]
    if n_hi:
        wlo_ref, out_refs = refs[3], refs[4:]
    else:
        out_refs = refs[3:]
    xn = _rms(x_ref[...], nw_ref[...])
    if mean_rows:
        out_refs, xm_ref = out_refs[:-1], out_refs[-1]
        for g in range(xn.shape[0] // mean_rows):
            xm_ref[0, g:g + 1, :] = jnp.mean(xn[g * mean_rows:(g + 1) * mean_rows, :], axis=0, keepdims=True)
    xh = xn.astype(BF16)
    if n_hi:
        xl = (xn - xh.astype(F32)).astype(BF16)
    c0 = 0
    for oi, (o_ref, width) in enumerate(zip(out_refs, splits)):
        for s in range(0, width, chunk):
            wd = min(chunk, width - s)
            a, b = c0 + s, c0 + s + wd
            acc = _dot(xh, w_ref[:, a:b])
            if b <= n_hi:
                acc = acc + (_dot(xl, w_ref[:, a:b]) + _dot(xh, wlo_ref[:, a:b]))
            if oi in head_major:
                o_ref[0] = acc.T.reshape(o_ref.shape[1:])
            else:
                o_ref[:, s:s + wd] = acc
        c0 += width


def _norm_matmul(x, norm_w, w_hi, w_lo, splits, tm, head_major=(), seq_len=None, head_dim=None, mean_rows=0):
    m, k = x.shape
    n_hi = 0 if w_lo is None else w_lo.shape[1]
    assert m % tm == 0 and sum(splits) == w_hi.shape[1]
    in_specs = [pl.BlockSpec((tm, k), lambda i: (i, 0)), _const_spec((1, k)), _const_spec(w_hi.shape)]
    args = [x, norm_w.reshape(1, k), w_hi]
    if n_hi:
        in_specs.append(_const_spec(w_lo.shape))
        args.append(w_lo)
    chunk = 512
    out_specs, out_shape = [], []
    for oi, wd in enumerate(splits):
        if oi in head_major:
            assert wd <= chunk and wd % head_dim == 0 and seq_len % tm == 0 and m % seq_len == 0
            tiles = seq_len // tm
            out_specs.append(pl.BlockSpec((1, wd // head_dim, head_dim, tm), lambda i: (i // tiles, 0, 0, i % tiles)))
            out_shape.append(jax.ShapeDtypeStruct((m // seq_len, wd // head_dim, head_dim, seq_len), F32))
        else:
            out_specs.append(pl.BlockSpec((tm, wd), lambda i: (i, 0)))
            out_shape.append(jax.ShapeDtypeStruct((m, wd), F32))
    if mean_rows:
        assert tm % mean_rows == 0
        out_specs.append(pl.BlockSpec((1, tm // mean_rows, k), lambda i: (i, 0, 0)))
        out_shape.append(jax.ShapeDtypeStruct((m // tm, tm // mean_rows, k), F32))
    outs = pl.pallas_call(
        functools.partial(_norm_matmul_kernel, splits=tuple(splits), n_hi=n_hi, chunk=chunk,
                          head_major=tuple(head_major), mean_rows=mean_rows),
        grid=(m // tm,),
        in_specs=in_specs,
        out_specs=out_specs,
        out_shape=out_shape,
        compiler_params=_params(("parallel",)),
        name="norm_matmul",
    )(*args)
    if mean_rows:
        outs = list(outs[:-1]) + [outs[-1].reshape(m // mean_rows, k)]
    return outs


def _topk_select(gates, valid, colf, ncol, axis):
    gs = [jnp.where(valid, gate, -jnp.inf) for gate in gates]
    sels = [None] * len(gs)
    for _ in range(MOBA_TOPK):
        ms = [jnp.max(g, axis=axis, keepdims=True) for g in gs]
        idxs = [jnp.min(jnp.where(g == m, colf, float(ncol)), axis=axis, keepdims=True) for g, m in zip(gs, ms)]
        picks = [colf == idx for idx in idxs]
        sels = [pick if sel is None else jnp.logical_or(sel, pick) for sel, pick in zip(sels, picks)]
        gs = [jnp.where(pick, -jnp.inf, g) for g, pick in zip(gs, picks)]
    return [jnp.logical_and(sel, valid) for sel in sels]


AUG_ALIBI = 9
AUG_SEL0 = 16
V_ROWS = LANES + 16
LOG2E = 1.4426950408889634
MOBA_UNROLL = 4


def _moba_prompt_kernel(coef_ref, q_ref, k_ref, v_ref, xm_ref, wk_ref, o_ref, kaug_ref, vt_ref, kmp_ref, qat_ref, acc_ref,
                        st_ref, p_ref, al_ref, mx_ref, *, blk, nb):
    hp = pl.program_id(1)
    i = pl.program_id(2)
    d = MOBA_HEAD_DIM
    i_f = jnp.asarray(i, F32)

    @pl.when(i == 0)
    def _():
        kmp_ref[0:nb, :] = _dot3(xm_ref[0], wk_ref[...])
        if kmp_ref.shape[0] > nb:
            kmp_ref[nb:, :] = jnp.zeros((kmp_ref.shape[0] - nb, LANES), F32)
        lane = lax.broadcasted_iota(jnp.int32, (blk, LANES), 1)
        rowf = lax.broadcasted_iota(jnp.int32, (blk, LANES), 0).astype(F32)

        for j in range(nb):
            cols = slice(j * blk, (j + 1) * blk)
            kj = k_ref[0, :, :, cols].reshape(LANES, blk).T
            aug = jnp.where(lane < 3, float(j), jnp.where(lane < 6, 1.0, jnp.where(lane < AUG_ALIBI, rowf, 0.0)))
            aug = jnp.where(lane == AUG_SEL0 + j, 1.0, aug)
            kaug_ref[j, :, 0:LANES] = kj.astype(BF16)
            kaug_ref[j, :, LANES:2 * LANES] = aug.astype(BF16)
            vt_ref[j, 0:LANES, :] = v_ref[0, :, :, cols].reshape(LANES, blk).astype(BF16)
            vt_ref[j, LANES:V_ROWS, :] = jnp.ones((V_ROWS - LANES, blk), BF16)

    qt = q_ref[0].T
    row = lax.broadcasted_iota(jnp.int32, (LANES, blk), 0)
    nbr = kmp_ref.shape[0]
    brow = lax.broadcasted_iota(jnp.int32, (nbr, blk), 0)
    browf = brow.astype(F32)
    crow = lax.broadcasted_iota(jnp.int32, (AUG_SEL0, blk), 0)
    kmp = kmp_ref[...]
    qhs = [jnp.where((row >= d * h2) & (row < d * (h2 + 1)), qt, 0.0) for h2 in range(2)]
    gates = [_dot3(kmp, qh) for qh in qhs]
    sels = _topk_select(gates, brow < i, browf, nbr, 0)
    for h2 in range(2):
        head = 2 * hp + h2
        selbias = jnp.where(jnp.logical_or(sels[h2], brow == i), 0.0, NEG)
        coef = jnp.zeros((AUG_SEL0, blk), F32)
        c_blk = jnp.full((1, blk), -(coef_ref[head, 6] * i_f), F32)
        for li, part in zip((3, 4, 5), _split3(c_blk)):
            coef = jnp.where(crow == li, part.astype(F32), coef)
        for li, ci in ((0, 0), (1, 1), (2, 2), (6, 3), (7, 4), (8, 5)):
            coef = jnp.where(crow == li, coef_ref[head, ci], coef)
        qat_ref[h2, 0:LANES, :] = (qhs[h2] * (d ** -0.5 * LOG2E)).astype(BF16)
        qat_ref[h2, LANES:LANES + AUG_SEL0, :] = coef.astype(BF16)
        qat_ref[h2, LANES + AUG_SEL0:LANES + AUG_SEL0 + nbr, :] = selbias.astype(BF16)
        qat_ref[h2, LANES + AUG_SEL0 + nbr:, :] = jnp.zeros((LANES - AUG_SEL0 - nbr, blk), BF16)

    def stage_a(slot, kb):
        for h2 in range(2):
            st = _dot(kaug_ref[kb], qat_ref[h2])
            st_ref[slot, h2] = st
            mx_ref[slot, h2] = jnp.max(st, axis=0, keepdims=True)

    def stage_b(slot, ms):
        new = []
        for h2 in range(2):
            m_new = jnp.maximum(ms[h2], mx_ref[slot, h2])
            al_ref[slot, h2] = jnp.exp2(ms[h2] - m_new)
            p_ref[slot, h2] = jnp.exp2(st_ref[slot, h2] - m_new).astype(BF16)
            new.append(m_new)
        return tuple(new)

    def stage_c(slot, kb):
        for h2 in range(2):
            acc_ref[h2] = al_ref[slot, h2] * acc_ref[h2] + _dot(vt_ref[kb], p_ref[slot, h2])

    acc_ref[...] = jnp.zeros_like(acc_ref)
    p_ref[1] = jnp.zeros(p_ref.shape[1:], BF16)
    al_ref[1] = jnp.ones(al_ref.shape[1:], F32)
    keyi = lax.broadcasted_iota(jnp.int32, (blk, blk), 0)
    qryi = lax.broadcasted_iota(jnp.int32, (blk, blk), 1)
    for h2 in range(2):
        st = jnp.where(keyi <= qryi, _dot(kaug_ref[i], qat_ref[h2]), NEG)
        st_ref[0, h2] = st
        mx_ref[0, h2] = jnp.max(st, axis=0, keepdims=True)
    ms = (jnp.full((1, blk), NEG, F32),) * 2

    def step(s, slot, ms):
        stage_c(slot, jnp.where(s == 2, i, jnp.maximum(s - 3, 0)))
        ms = stage_b(1 - slot, ms)
        stage_a(slot, s - 1)
        return ms

    def unrolled(t, ms):
        for u in range(MOBA_UNROLL):
            ms = step(MOBA_UNROLL * t + 1 + u, (1 + u) % 2, ms)
        return ms

    def pair(t, ms):
        return step(t + 1, 0, step(t, 1, ms))

    n_full = i // MOBA_UNROLL
    ms = lax.fori_loop(0, n_full, unrolled, ms)
    done = MOBA_UNROLL * n_full + 1
    n_pair = (i + 1 - done) // 2
    ms = lax.fori_loop(0, n_pair, lambda t, ms: pair(done + 2 * t, ms), ms)
    ms = lax.fori_loop(done + 2 * n_pair, i + 1, lambda s, ms: step(s, s & 1, ms), ms)
    last = i & 1
    stage_c(1 - last, jnp.where(i == 1, i, jnp.maximum(i - 2, 0)))
    stage_b(last, ms)
    stage_c(last, jnp.maximum(i - 1, 0))

    outs = []
    for h2 in range(2):
        acc = acc_ref[h2]
        outs.append(acc[0:LANES, :] / acc[LANES:LANES + 1, :])
    o_ref[0] = jnp.where(row < d, outs[0], outs[1]).T


def _moba_prompt(q, kt, vt, xmean, w_in, k_col0, slopes):
    b, t, w = q.shape
    blk = MOBA_BLOCK
    assert t % blk == 0 and w == MOBA_WIDTH and blk == 2 * LANES
    assert kt.shape == (b, MOBA_HEADS, MOBA_HEAD_DIM, t) and 2 * MOBA_HEAD_DIM == LANES
    nb = t // blk
    nbr = -(-nb // 16) * 16
    assert AUG_SEL0 + nbr < LANES and nb <= 256 and AUG_ALIBI <= AUG_SEL0 and AUG_SEL0 % 16 == 0
    assert xmean.shape[:2] == (b, nb) and k_col0 % LANES == 0
    c1 = slopes * LOG2E
    c256 = c1 * blk
    coef = jnp.stack([p.astype(F32) for p in _split3(c256)] + [p.astype(F32) for p in _split3(c1)]
                     + [c256, jnp.zeros_like(c1)], axis=1)
    kv_spec = pl.BlockSpec((1, 2, MOBA_HEAD_DIM, t), lambda bi, hp, i: (bi, hp, 0, 0))
    q_spec = pl.BlockSpec((1, blk, LANES), lambda bi, hp, i: (bi, i, hp))
    return pl.pallas_call(
        functools.partial(_moba_prompt_kernel, blk=blk, nb=nb),
        grid=(b, w // LANES, nb),
        in_specs=[pl.BlockSpec(memory_space=pltpu.SMEM), q_spec, kv_spec, kv_spec,
                  pl.BlockSpec((1, nb, xmean.shape[2]), lambda bi, hp, i: (bi, 0, 0)),
                  pl.BlockSpec((w_in.shape[0], LANES), lambda bi, hp, i: (0, k_col0 // LANES + hp))],
        out_specs=q_spec,
        out_shape=jax.ShapeDtypeStruct((b, t, w), F32),
        scratch_shapes=[pltpu.VMEM((nb, blk, 2 * LANES), BF16),
                        pltpu.VMEM((nb, V_ROWS, blk), BF16),
                        pltpu.VMEM((nbr, LANES), F32),
                        pltpu.VMEM((2, 2 * LANES, blk), BF16),
                        pltpu.VMEM((2, V_ROWS, blk), F32),
                        pltpu.VMEM((2, 2, blk, blk), F32),
                        pltpu.VMEM((2, 2, blk, blk), BF16),
                        pltpu.VMEM((2, 2, 1, blk), F32),
                        pltpu.VMEM((2, 2, 1, blk), F32)],
        compiler_params=_params(("parallel", "parallel", "arbitrary")),
        name="moba_prompt",
    )(coef, q, kt, vt, xmean, w_in)


def _tri_inverses(lows, c):
    row = lax.broadcasted_iota(jnp.int32, (c, c), 0)
    col = lax.broadcasted_iota(jnp.int32, (c, c), 1)
    eye = jnp.where(row == col, 1.0, 0.0)
    ts = [eye - low for low in lows]
    ps = [low.astype(BF16) for low in lows]
    for _ in range(int(math.log2(c)) - 1):
        ps = [_dot(p, p).astype(BF16) for p in ps]
        ts = [t + _dot(t.astype(BF16), p) for t, p in zip(ts, ps)]
    return ts


def _gdn_prompt_kernel(x_ref, z_ref, ab_ref, cw_ref, alog_ref, dtb_ref, nw_ref,
                       o_ref, s_out_ref, cb_out_ref,
                       xext_ref, s_ref, qn_ref, kn_ref, vv_ref, gc_ref, beta_ref,
                       el_ref, m_ref, n_ref, qe_ref, o0_ref, *, tt, c):
    t_idx = pl.program_id(1)
    nt = pl.num_programs(1)
    hd = GDN_HEAD_DIM
    gw = GDN_WIDTH
    halo = SUBLANES

    @pl.when(t_idx == 0)
    def _():
        s_ref[...] = jnp.zeros_like(s_ref)
        xext_ref[0:halo, :] = jnp.zeros((halo, 3 * gw), F32)

    xext_ref[halo:halo + tt, :] = x_ref[0]
    conv = xext_ref[halo:halo + tt, :] * cw_ref[GDN_CONV - 1:GDN_CONV, :]
    for j in range(GDN_CONV - 1):
        off = halo - (GDN_CONV - 1) + j
        conv = conv + xext_ref[off:off + tt, :] * cw_ref[j:j + 1, :]

    @pl.when(t_idx == nt - 1)
    def _():
        cb_out_ref[0] = xext_ref[halo + tt - (GDN_CONV - 1):halo + tt, :]

    xext_ref[0:halo, :] = xext_ref[tt:tt + halo, :]

    act = _silu(conv)
    for h in range(GDN_HEADS):
        qh = act[:, h * hd:(h + 1) * hd]
        kh = act[:, gw + h * hd:gw + (h + 1) * hd]
        qn_ref[:, h * hd:(h + 1) * hd] = qh * lax.rsqrt(jnp.sum(qh * qh, axis=-1, keepdims=True) + L2_EPS) * (hd ** -0.5)
        kn_ref[:, h * hd:(h + 1) * hd] = kh * lax.rsqrt(jnp.sum(kh * kh, axis=-1, keepdims=True) + L2_EPS)
    vv_ref[...] = act[:, 2 * gw:]

    abv = ab_ref[0]
    g_all = -jnp.exp(alog_ref[...]) * _softplus(abv + dtb_ref[...])
    beta_ref[...] = _sigmoid(abv)
    rc = lax.broadcasted_iota(jnp.int32, (c, c), 0)
    cc = lax.broadcasted_iota(jnp.int32, (c, c), 1)
    tri = jnp.where(rc >= cc, 1.0, 0.0).astype(BF16)
    for ci in range(tt // c):
        gc_ref[ci * c:(ci + 1) * c, :] = _dot_exact_lhs(tri, g_all[ci * c:(ci + 1) * c, :])

    incl = rc >= cc
    strict = rc > cc
    lane8 = lax.broadcasted_iota(jnp.int32, (SUBLANES, LANES), 1)
    nw = nw_ref[...]

    def prepare_group(gi, carry):
        probs = []
        for cj in range(GDN_GROUP):
            ci = gi * GDN_GROUP + cj
            r0 = pl.multiple_of(ci * c, c)
            gcs = gc_ref[pl.ds(r0, c), :]
            bet = beta_ref[pl.ds(r0, c), :]
            e_gc = jnp.exp(gcs)
            g_last = gcs[c - 1:c, :]
            e_rev = jnp.exp(g_last - gcs)
            el_ref[pl.ds(ci, 1), :] = jnp.exp(g_last)
            for h in range(GDN_HEADS):
                hs = slice(h * hd, (h + 1) * hd)
                onehot = jnp.where(lane8 == h, 1.0, 0.0).astype(BF16)
                grow = _dot_exact_lhs(onehot, gcs, _dot_nt)[0:1, :]
                probs.append(dict(
                    idx=ci * GDN_HEADS + h, qn=qn_ref[pl.ds(r0, c), hs], kn=kn_ref[pl.ds(r0, c), hs],
                    vv=vv_ref[pl.ds(r0, c), hs], bcol=bet[:, GDN_HEADS + h:GDN_HEADS + h + 1],
                    egc=e_gc[:, h:h + 1], erev=e_rev[:, h:h + 1],
                    decay=jnp.where(incl, jnp.exp(gcs[:, h:h + 1] - grow), 0.0)))
        for p in probs:
            p["knb"] = p["kn"].astype(BF16)
        lows = [jnp.where(strict, p["bcol"] * _dot_nt(p["knb"], p["knb"]) * p["decay"], 0.0) for p in probs]
        tinvs = _tri_inverses(lows, c)
        sols = []
        for p, tinv in zip(probs, tinvs):
            rhs = jnp.concatenate([p["kn"] * (p["bcol"] * p["egc"]), p["vv"] * p["bcol"]], axis=1)
            sols.append(_dot(tinv.astype(BF16), rhs.astype(BF16)).astype(BF16))
        for p, sol in zip(probs, sols):
            aqk = (_dot_nt(p["qn"].astype(BF16), p["knb"]) * p["decay"]).astype(BF16)
            upd = _dot_tn((p["kn"] * p["erev"]).astype(BF16), sol)
            out = _dot(aqk, sol)
            m_ref[p["idx"]] = upd[:, 0:hd].astype(BF16)
            n_ref[p["idx"]] = upd[:, hd:]
            qe_ref[p["idx"]] = (p["qn"] * p["egc"] - out[:, 0:hd]).astype(BF16)
            o0_ref[p["idx"]] = out[:, hd:]
        return carry

    lax.fori_loop(0, tt // (GDN_GROUP * c), prepare_group, 0)

    def advance(ci, carry):
        r0 = pl.multiple_of(ci * c, c)
        e_last = el_ref[pl.ds(ci, 1), :]
        for h in range(GDN_HEADS):
            hs = slice(h * hd, (h + 1) * hd)
            idx = ci * GDN_HEADS + h
            s = s_ref[h]
            sb = s.astype(BF16)
            o = _dot(qe_ref[idx], sb) + o0_ref[idx]
            s_ref[h] = s * e_last[:, h:h + 1] - _dot(m_ref[idx], sb) + n_ref[idx]
            o = o * lax.rsqrt(jnp.mean(o * o, axis=-1, keepdims=True) + RMS_EPS) * nw
            o_ref[0, pl.ds(r0, c), hs] = o * _silu(z_ref[0, pl.ds(r0, c), hs])
        return carry

    lax.fori_loop(0, tt // c, advance, 0)

    @pl.when(t_idx == nt - 1)
    def _():
        s_out_ref[0] = s_ref[...]


def _gdn_prompt(gqkv, z, ab, conv_w, alog_pad, dtb_pad, norm_w, tt):
    b, t, w3 = gqkv.shape
    c = GDN_CHUNK
    assert t % tt == 0 and tt % (GDN_GROUP * c) == 0 and w3 == 3 * GDN_WIDTH
    hd = GDN_HEAD_DIM
    nch = tt // c
    tile = lambda wd: pl.BlockSpec((1, tt, wd), lambda bi, ti: (bi, ti, 0))
    return pl.pallas_call(
        functools.partial(_gdn_prompt_kernel, tt=tt, c=c),
        grid=(b, t // tt),
        in_specs=[tile(w3), tile(GDN_WIDTH), tile(LANES), _const_spec(conv_w.shape),
                  _const_spec((1, LANES)), _const_spec((1, LANES)), _const_spec((1, hd))],
        out_specs=[tile(GDN_WIDTH),
                   pl.BlockSpec((1, GDN_HEADS, hd, hd), lambda bi, ti: (bi, 0, 0, 0)),
                   pl.BlockSpec((1, GDN_CONV - 1, w3), lambda bi, ti: (bi, 0, 0))],
        out_shape=[jax.ShapeDtypeStruct((b, t, GDN_WIDTH), F32),
                   jax.ShapeDtypeStruct((b, GDN_HEADS, hd, hd), F32),
                   jax.ShapeDtypeStruct((b, GDN_CONV - 1, w3), F32)],
        scratch_shapes=[pltpu.VMEM((tt + SUBLANES, w3), F32),
                        pltpu.VMEM((GDN_HEADS, hd, hd), F32),
                        pltpu.VMEM((tt, GDN_WIDTH), F32),
                        pltpu.VMEM((tt, GDN_WIDTH), F32),
                        pltpu.VMEM((tt, GDN_WIDTH), F32),
                        pltpu.VMEM((tt, LANES), F32),
                        pltpu.VMEM((tt, LANES), F32),
                        pltpu.VMEM((nch, LANES), F32),
                        pltpu.VMEM((nch * GDN_HEADS, hd, hd), BF16),
                        pltpu.VMEM((nch * GDN_HEADS, hd, hd), F32),
                        pltpu.VMEM((nch * GDN_HEADS, c, hd), BF16),
                        pltpu.VMEM((nch * GDN_HEADS, c, hd), F32)],
        compiler_params=_params(("parallel", "arbitrary")),
        name="gdn_prompt",
    )(gqkv, z, ab, conv_w, alog_pad, dtb_pad, norm_w.reshape(1, hd))


def _gdn_sample_kernel(x_ref, z_ref, ab_ref, cb_ref, s_in_ref, cw_ref, alog_ref, dtb_ref, nw_ref,
                       o_ref, s_out_ref, cb_out_ref):
    hd = GDN_HEAD_DIM
    gw = GDN_WIDTH
    nc = GDN_CONV - 1
    x = x_ref[0]
    cb = cb_ref[0]
    conv = x * cw_ref[nc:nc + 1, :]
    for j in range(nc):
        conv = conv + cb[j:j + 1, :] * cw_ref[j:j + 1, :]
    cb_out_ref[0, 0:nc - 1, :] = cb[1:nc, :]
    cb_out_ref[0, nc - 1:nc, :] = x
    act = _silu(conv)
    abv = ab_ref[0]
    g_all = -jnp.exp(alog_ref[...]) * _softplus(abv + dtb_ref[...])
    e_g = jnp.exp(g_all)
    beta = _sigmoid(abv)
    z = z_ref[0]
    nw = nw_ref[...]
    eye = (lax.broadcasted_iota(jnp.int32, (hd, hd), 0) == lax.broadcasted_iota(jnp.int32, (hd, hd), 1))

    def as_column(row):
        return jnp.sum(jnp.where(eye, row, 0.0), axis=1, keepdims=True)

    heads = range(GDN_HEADS)
    hss = [slice(h * hd, (h + 1) * hd) for h in heads]
    qhs = [act[:, hs] for hs in hss]
    khs = [act[:, gw + h * hd:gw + (h + 1) * hd] for h in heads]
    vvs = [act[:, 2 * gw + h * hd:2 * gw + (h + 1) * hd] for h in heads]
    qns = [qh * lax.rsqrt(jnp.sum(qh * qh, axis=-1, keepdims=True) + L2_EPS) * (hd ** -0.5) for qh in qhs]
    kns = [kh * lax.rsqrt(jnp.sum(kh * kh, axis=-1, keepdims=True) + L2_EPS) for kh in khs]
    egs = [e_g[:, h:h + 1] for h in heads]
    bhs = [beta[:, GDN_HEADS + h:GDN_HEADS + h + 1] for h in heads]
    ss = [s_in_ref[0, h] for h in heads]
    k_cols = [as_column(kn) for kn in kns]
    q_cols = [as_column(qn) for qn in qns]
    v_news = [vvs[h] * bhs[h] - jnp.sum((k_cols[h] * (bhs[h] * egs[h])) * ss[h], axis=0, keepdims=True) for h in heads]
    qks = [jnp.sum(qns[h] * kns[h], axis=-1, keepdims=True) for h in heads]
    os_ = [jnp.sum((q_cols[h] * egs[h]) * ss[h], axis=0, keepdims=True) + qks[h] * v_news[h] for h in heads]
    for h in heads:
        s_out_ref[0, h] = ss[h] * egs[h] + k_cols[h] * v_news[h]
    for h in heads:
        o = os_[h] * lax.rsqrt(jnp.mean(os_[h] * os_[h], axis=-1, keepdims=True) + RMS_EPS) * nw
        o_ref[0, :, hss[h]] = o * _silu(z[:, hss[h]])


def _gdn_sample(gqkv, z, ab, conv_buf, s0, conv_w, alog_pad, dtb_pad, norm_w):
    db, w3 = gqkv.shape
    hd = GDN_HEAD_DIM
    nc = GDN_CONV - 1
    row = lambda wd: pl.BlockSpec((1, 1, wd), lambda bi: (bi, 0, 0))
    st = pl.BlockSpec((1, GDN_HEADS, hd, hd), lambda bi: (bi, 0, 0, 0))
    cbs = pl.BlockSpec((1, nc, w3), lambda bi: (bi, 0, 0))
    o, s_new, cb_new = pl.pallas_call(
        _gdn_sample_kernel,
        grid=(db,),
        in_specs=[row(w3), row(GDN_WIDTH), row(LANES), cbs, st, _const_spec(conv_w.shape),
                  _const_spec((1, LANES)), _const_spec((1, LANES)), _const_spec((1, hd))],
        out_specs=[row(GDN_WIDTH), st, cbs],
        out_shape=[jax.ShapeDtypeStruct((db, 1, GDN_WIDTH), F32),
                   jax.ShapeDtypeStruct((db, GDN_HEADS, hd, hd), F32),
                   jax.ShapeDtypeStruct((db, nc, w3), F32)],
        compiler_params=_params(("parallel",)),
        name="gdn_sample",
    )(gqkv.reshape(db, 1, w3), z.reshape(db, 1, GDN_WIDTH), ab.reshape(db, 1, LANES), conv_buf, s0,
      conv_w, alog_pad, dtb_pad, norm_w.reshape(1, hd))
    return o.reshape(db, GDN_WIDTH), s_new, cb_new


DMA_RING = 8
ROUTE_BANKS = 4


def _moba_route_kernel(pt_ref, q_ref, pool_ref, o_ref, kbuf, sem, qc_ref, *, n_pages, ppb):
    b = pl.program_id(0)
    n_seq = pl.num_programs(0)
    n_groups = n_pages // DMA_RING
    bpg = DMA_RING // ppb
    nblk = n_pages // ppb
    hh, d = MOBA_HEADS, MOBA_HEAD_DIM

    def page_copy(seq, g, u, bank):
        page = pt_ref[seq * n_pages + g * DMA_RING + u]
        return pltpu.make_async_copy(pool_ref.at[page], kbuf.at[bank, u], sem.at[bank, u])

    def start_group(seq, g, bank):
        for u in range(DMA_RING):
            page_copy(seq, g, u, bank).start()

    ahead = ROUTE_BANKS - 1

    @pl.when(b == 0)
    def _():
        for g0 in range(ahead):
            start_group(b, g0, g0)

    q = q_ref[0]
    eye = lax.broadcasted_iota(jnp.int32, (d, d), 0) == lax.broadcasted_iota(jnp.int32, (d, d), 1)
    for h in range(hh):
        q_col = jnp.sum(jnp.where(eye, q[h:h + 1, :], 0.0), axis=1, keepdims=True)
        qc_ref[h] = jnp.broadcast_to(q_col, (d, LANES))
    lane = lax.broadcasted_iota(jnp.int32, (hh, LANES), 1)
    rowi = lax.broadcasted_iota(jnp.int32, (hh, LANES), 0)

    def group(g, gate):
        bank = g & (ROUTE_BANKS - 1)
        for u in range(DMA_RING):
            page_copy(b, g, u, bank).wait()

        free_bank = (g + ahead) & (ROUTE_BANKS - 1)

        @pl.when(g + ahead < n_groups)
        def _():
            start_group(b, g + ahead, free_bank)

        @pl.when(jnp.logical_and(g + ahead >= n_groups, b + 1 < n_seq))
        def _():
            start_group(b + 1, g + ahead - n_groups, free_bank)

        for kb in range(bpg):
            x = kbuf[bank, kb * ppb]
            for u in range(1, ppb):
                x = x + kbuf[bank, kb * ppb + u]
            w = (x * qc_ref[...]).reshape(hh, d // SUBLANES, SUBLANES, LANES)
            r = jnp.sum(w, axis=1)
            per_head = jnp.zeros((hh, LANES), F32)
            for h in range(hh):
                per_head = jnp.where(rowi == h, jnp.sum(r[h], axis=0, keepdims=True), per_head)
            total = jnp.sum(per_head, axis=1, keepdims=True)
            gate = jnp.where(lane == g * bpg + kb, total, gate)
        return gate

    gate = lax.fori_loop(0, n_groups, group, jnp.zeros((hh, LANES), F32)) * (1.0 / MOBA_BLOCK)
    lanef = lane.astype(F32)
    gate = jnp.where(lane < nblk, gate, -jnp.inf)
    out = jnp.zeros((hh, LANES), F32)
    for r in range(MOBA_TOPK):
        m = jnp.max(gate, axis=1, keepdims=True)
        idx = jnp.min(jnp.where(gate == m, lanef, float(LANES)), axis=1, keepdims=True)
        gate = jnp.where(lanef == idx, -jnp.inf, gate)
        out = jnp.where(lane == r, idx, out)
    o_ref[0] = out.astype(jnp.int32)


def _moba_route(q3, pool_kt, pt_flat, n_pages, ppb):
    db = q3.shape[0]
    _, hh, d, ps = pool_kt.shape
    assert n_pages % (ROUTE_BANKS * DMA_RING) == 0 and ROUTE_BANKS & (ROUTE_BANKS - 1) == 0
    assert DMA_RING % ppb == 0 and ps == LANES and n_pages // ppb <= LANES
    return pl.pallas_call(
        functools.partial(_moba_route_kernel, n_pages=n_pages, ppb=ppb),
        grid_spec=pltpu.PrefetchScalarGridSpec(
            num_scalar_prefetch=1,
            grid=(db,),
            in_specs=[pl.BlockSpec((1, hh, d), lambda b, pt: (b, 0, 0)), pl.BlockSpec(memory_space=pl.ANY)],
            out_specs=pl.BlockSpec((1, hh, LANES), lambda b, pt: (b, 0, 0)),
            scratch_shapes=[pltpu.VMEM((ROUTE_BANKS, DMA_RING, hh, d, ps), F32),
                            pltpu.SemaphoreType.DMA((ROUTE_BANKS, DMA_RING)),
                            pltpu.VMEM((hh, d, LANES), F32)],
        ),
        out_shape=jax.ShapeDtypeStruct((db, hh, LANES), jnp.int32),
        compiler_params=_params(("arbitrary",)),
        name="moba_sample_route",
    )(pt_flat, q3, pool_kt)


def _moba_sample_attn_kernel(pg_ref, bk_ref, slope_ref, q_ref, kn_ref, vn_ref, pk_ref, pv_ref, o_ref,
                             kbuf, vbuf, ksem, vsem, *, per_head, ppb, ps, past):
    b = pl.program_id(0)
    n_seq = pl.num_programs(0)
    hh, d = MOBA_HEADS, MOBA_HEAD_DIM
    n_slab = hh * per_head

    def copies(seq, t, bank):
        pg = pg_ref[seq * n_slab + t]
        h = t // per_head
        return (pltpu.make_async_copy(pk_ref.at[pg, h], kbuf.at[bank, t], ksem.at[bank, t]),
                pltpu.make_async_copy(pv_ref.at[pg, h], vbuf.at[bank, t], vsem.at[bank, t]))

    def start_all(seq, bank):
        for t in range(n_slab):
            for c in copies(seq, t, bank):
                c.start()

    @pl.when(b == 0)
    def _():
        start_all(b, 0)

    bank = b & 1
    for t in range(n_slab):
        for c in copies(b, t, bank):
            c.wait()

    @pl.when(b + 1 < n_seq)
    def _():
        start_all(b + 1, 1 - bank)

    q = q_ref[0] * (d ** -0.5)
    kn = kn_ref[0]
    vn = vn_ref[0]
    lanef = lax.broadcasted_iota(jnp.int32, (1, ps), 1).astype(F32)
    qhs = [jnp.broadcast_to(q[h:h + 1, :], (SUBLANES, d)).astype(BF16) for h in range(hh)]
    scores = [_dot(qhs[t // per_head], kbuf[bank, t].astype(BF16))[0:1, :] for t in range(n_slab)]
    s_alls, s_news = [], []
    for h in range(hh):
        rows = []
        for j in range(per_head):
            t = h * per_head + j
            blk = bk_ref[b * (n_slab // ppb) + t // ppb]
            pos0 = blk * (ppb * ps) + (t % ppb) * ps
            rows.append(scores[t] - slope_ref[h] * (jnp.asarray(past - pos0, F32) - lanef))
        s_alls.append(jnp.concatenate(rows, axis=1))
        s_news.append(jnp.sum(q[h:h + 1, :] * kn[h:h + 1, :], axis=1, keepdims=True))
    ms = [jnp.maximum(jnp.max(s_all, axis=1, keepdims=True), s_new) for s_all, s_new in zip(s_alls, s_news)]
    ps_ = [jnp.exp(s_all - m) for s_all, m in zip(s_alls, ms)]
    p_news = [jnp.exp(s_new - m) for s_new, m in zip(s_news, ms)]
    outs = []
    for h in range(hh):
        v_all = jnp.concatenate([vbuf[bank, h * per_head + j].astype(BF16) for j in range(per_head)], axis=1)
        p8 = jnp.broadcast_to(ps_[h], (SUBLANES, ps_[h].shape[1])).astype(BF16)
        outs.append(_dot_nt(p8, v_all)[0:1, :])
    for h in range(hh):
        l = jnp.sum(ps_[h], axis=1, keepdims=True) + p_news[h]
        o_ref[0, h:h + 1, :] = (outs[h] + p_news[h] * vn[h:h + 1, :]) / l


def _moba_sample_attn(q3, kn3, vn3, pool_kt, pool_vt, pages_flat, blocks_flat, slopes, per_head, ppb, past):
    db, hh, d = q3.shape
    _, _, _, ps = pool_kt.shape
    n_slab = hh * per_head
    row = pl.BlockSpec((1, hh, d), lambda b, pg, bk: (b, 0, 0))
    hbm = pl.BlockSpec(memory_space=pl.ANY)
    return pl.pallas_call(
        functools.partial(_moba_sample_attn_kernel, per_head=per_head, ppb=ppb, ps=ps, past=past),
        grid_spec=pltpu.PrefetchScalarGridSpec(
            num_scalar_prefetch=2,
            grid=(db,),
            in_specs=[pl.BlockSpec(memory_space=pltpu.SMEM), row, row, row, hbm, hbm],
            out_specs=row,
            scratch_shapes=[pltpu.VMEM((2, n_slab, d, ps), F32), pltpu.VMEM((2, n_slab, d, ps), F32),
                            pltpu.SemaphoreType.DMA((2, n_slab)), pltpu.SemaphoreType.DMA((2, n_slab))],
        ),
        out_shape=jax.ShapeDtypeStruct((db, hh, d), F32),
        compiler_params=_params(("arbitrary",)),
        name="moba_sample_attn",
    )(pages_flat, blocks_flat, slopes, q3, kn3, vn3, pool_kt, pool_vt)


def _moba_sample(q, k_new, v_new, pool_k, pool_v, page_table, slopes):
    db, n_pages = page_table.shape
    _, ps, hh, d = pool_k.shape
    assert MOBA_BLOCK % ps == 0 and hh == MOBA_HEADS and d == MOBA_HEAD_DIM
    ppb = MOBA_BLOCK // ps
    past = n_pages * ps
    assert past % MOBA_BLOCK == 0 and past // MOBA_BLOCK >= MOBA_TOPK
    pool_kt = jnp.transpose(pool_k, (0, 2, 3, 1))
    pool_vt = jnp.transpose(pool_v, (0, 2, 3, 1))
    q3 = q.reshape(db, hh, d)
    blocks = _moba_route(q3, pool_kt, page_table.reshape(-1), n_pages, ppb)[:, :, :MOBA_TOPK]
    pidx = (blocks[..., None] * ppb + jnp.arange(ppb, dtype=jnp.int32)).reshape(db, hh * MOBA_TOPK * ppb)
    pages = jnp.take_along_axis(page_table, pidx, axis=1)
    o = _moba_sample_attn(q3, k_new.reshape(db, hh, d), v_new.reshape(db, hh, d), pool_kt, pool_vt,
                          pages.reshape(-1), blocks.reshape(-1), slopes, MOBA_TOPK * ppb, ppb, past)
    return o.reshape(db, hh * d)


def _memory_attention(q, mk_ref, mv_ref, o_ref, hd):
    for h in range(X_HEADS):
        hs = slice(h * hd, (h + 1) * hd)
        qh = (q[:, hs] * (hd ** -0.5)).astype(BF16)
        s = _dot_nt(qh, mk_ref[0, :, hs].astype(BF16))
        m = jnp.max(s, axis=1, keepdims=True)
        p = jnp.exp(s - m)
        l = jnp.sum(p, axis=1, keepdims=True)
        o_ref[:, hs] = _dot(p.astype(BF16), mv_ref[0, :, hs].astype(BF16)) / l


def _mid_kernel(om_ref, og_ref, x_ref, wo_ref, nw_ref, wq_ref, *rest, attend):
    half = om_ref.shape[1]
    h = x_ref[...] + (_dot(om_ref[...].astype(BF16), wo_ref[0:half, :])
                      + _dot(og_ref[...].astype(BF16), wo_ref[half:, :]))
    q = _dot(_rms(h, nw_ref[...]).astype(BF16), wq_ref[...])
    if attend:
        mk_ref, mv_ref, h_ref, o_ref = rest
        _memory_attention(q, mk_ref, mv_ref, o_ref, q.shape[1] // X_HEADS)
    else:
        h_ref, q_ref = rest
        q_ref[...] = q
    h_ref[...] = h


def _mid(o_moba, o_gdn, x, w_out, norm_w, w_xq, tm, memory=None):
    m, dm = x.shape
    half = o_moba.shape[1]
    xw = w_xq.shape[1]
    assert m % tm == 0
    tile = lambda wd: pl.BlockSpec((tm, wd), lambda i: (i, 0))
    in_specs = [tile(half), tile(o_gdn.shape[1]), tile(dm), _const_spec(w_out.shape), _const_spec((1, dm)),
                _const_spec(w_xq.shape)]
    args = [o_moba, o_gdn, x, w_out, norm_w.reshape(1, dm), w_xq]
    if memory is not None:
        mk, mv = memory
        nbat, nm, _ = mk.shape
        assert m % nbat == 0 and (m // nbat) % tm == 0
        tiles = m // nbat // tm
        in_specs += [pl.BlockSpec((1, nm, xw), lambda i: (i // tiles, 0, 0))] * 2
        args += [mk, mv]
    return pl.pallas_call(
        functools.partial(_mid_kernel, attend=memory is not None),
        grid=(m // tm,),
        in_specs=in_specs,
        out_specs=[tile(dm), tile(xw)],
        out_shape=[jax.ShapeDtypeStruct((m, dm), F32), jax.ShapeDtypeStruct((m, xw), F32)],
        compiler_params=_params(("parallel",)),
        name="out_proj_xq",
    )(*args)


def _halving_reduce(x, op):
    n = x.shape[0]
    assert n & (n - 1) == 0
    while n > 1:
        n //= 2
        x = op(x[:n], x[n:])
    return x


def _xattn_sample_kernel(q_ref, mk_ref, mv_ref, o_ref, *, hd):
    q = q_ref[0] * (hd ** -0.5)
    s = jnp.sum(mk_ref[0] * q[None], axis=-1, keepdims=True)
    m = _halving_reduce(s, jnp.maximum)
    p = jnp.exp(s - m)
    l = _halving_reduce(p, jnp.add)
    o_ref[0] = _halving_reduce(p * mv_ref[0], jnp.add)[0] / l[0]


def _xattn_sample(qx, mk, mv):
    db, nm, hh, hd = mk.shape
    q_spec = pl.BlockSpec((1, hh, hd), lambda bi: (bi, 0, 0))
    m_spec = pl.BlockSpec((1, nm, hh, hd), lambda bi: (bi, 0, 0, 0))
    return pl.pallas_call(
        functools.partial(_xattn_sample_kernel, hd=hd),
        grid=(db,),
        in_specs=[q_spec, m_spec, m_spec],
        out_specs=q_spec,
        out_shape=jax.ShapeDtypeStruct((db, hh, hd), F32),
        compiler_params=_params(("parallel",)),
        name="xattn_sample",
    )(qx.reshape(db, hh, hd), mk, mv).reshape(db, hh * hd)


def _tail_kernel(h_ref, ox_ref, wxo_ref, nf_ref, wgu_ref, wd_ref, fn_ref, y_ref, *, d_ff, chunk):
    h = h_ref[...] + _dot(ox_ref[...].astype(BF16), wxo_ref[...])
    hn = _rms(h, nf_ref[...]).astype(BF16)
    acc = jnp.zeros(h.shape, F32)
    for c0 in range(0, d_ff, chunk):
        g = _dot(hn, wgu_ref[:, c0:c0 + chunk])
        u = _dot(hn, wgu_ref[:, d_ff + c0:d_ff + c0 + chunk])
        acc = acc + _dot((_silu(g) * u).astype(BF16), wd_ref[c0:c0 + chunk, :])
    y_ref[...] = _rms(h + acc, fn_ref[...])


def _tail(h, ox, w_xo, norm_ffn_w, w_gu, w_down, final_norm_w, tm):
    m, dm = h.shape
    d_ff = w_down.shape[0]
    chunk = 2 * LANES
    assert m % tm == 0 and d_ff % chunk == 0
    tile = lambda wd: pl.BlockSpec((tm, wd), lambda i: (i, 0))
    return pl.pallas_call(
        functools.partial(_tail_kernel, d_ff=d_ff, chunk=chunk),
        grid=(m // tm,),
        in_specs=[tile(dm), tile(ox.shape[1]), _const_spec(w_xo.shape), _const_spec((1, dm)),
                  _const_spec(w_gu.shape), _const_spec(w_down.shape), _const_spec((1, dm))],
        out_specs=tile(dm),
        out_shape=jax.ShapeDtypeStruct((m, dm), F32),
        compiler_params=_params(("parallel",)),
        name="xo_swiglu_norm",
    )(h, ox, w_xo, norm_ffn_w.reshape(1, dm), w_gu, w_down, final_norm_w.reshape(1, dm))


def _row_tile(m, pref):
    return pref if m % pref == 0 else m


def kernel(x_prompt, x_sample, cache_k, cache_v, page_table, state_conv, state_gdn, cache_mem_k, cache_mem_v, mem_prompt, norm_mix_w, w_in, conv_w, a_log, dt_bias, gdn_norm_w, w_out, norm_x_w, mem_norm_w, w_xq, w_xkv, w_xo, norm_ffn_w, w_gu, w_down, final_norm_w):
    assert w_in.shape[0] == 1, "one layer"
    b, t, dm = x_prompt.shape
    db, ds, _ = x_sample.shape
    assert ds == 1
    mw, gw = MOBA_WIDTH, GDN_WIDTH
    n_in = 3 * mw + 4 * gw + 2 * GDN_HEADS
    assert w_in.shape[2] == n_in

    w_in_p = jnp.pad(w_in[0], ((0, 0), (0, 3 * mw + 4 * gw + LANES - n_in)))
    w_in_hi = w_in_p.astype(BF16)
    w_in_lo = (w_in_p[:, :mw] - w_in_hi[:, :mw].astype(F32)).astype(BF16)
    in_splits = (mw, mw, mw, 3 * gw, gw, LANES)
    w_out_b, w_xq_b, w_xo_b = w_out[0].astype(BF16), w_xq[0].astype(BF16), w_xo[0].astype(BF16)
    w_gu_b, w_down_b, w_xkv_b = w_gu[0].astype(BF16), w_down[0].astype(BF16), w_xkv[0].astype(BF16)
    xw = w_xq.shape[2]
    pad_h = lambda v: jnp.pad(v.astype(F32), (0, LANES - GDN_HEADS)).reshape(1, LANES)
    alog_pad, dtb_pad = pad_h(a_log[0]), pad_h(dt_bias[0])
    slopes = jnp.exp2(-8.0 * jnp.arange(1, MOBA_HEADS + 1, dtype=F32) / MOBA_HEADS)

    mp = b * t
    xp = x_prompt.reshape(mp, dm)
    qm, kt, vt, gqkv, z, ab, xmean = _norm_matmul(xp, norm_mix_w[0], w_in_hi, w_in_lo, in_splits, _row_tile(t, 512),
                                                  head_major=(1, 2), seq_len=t, head_dim=MOBA_HEAD_DIM,
                                                  mean_rows=MOBA_BLOCK)
    mk_p, mv_p = _norm_matmul(mem_prompt.reshape(-1, dm), mem_norm_w[0], w_xkv_b, None, (xw, xw),
                              _row_tile(mem_prompt.shape[0] * mem_prompt.shape[1], 512))
    n_mem = mem_prompt.shape[1]
    o_moba = _moba_prompt(qm.reshape(b, t, mw), kt, vt, xmean.reshape(b, t // MOBA_BLOCK, dm), w_in[0], mw, slopes)
    o_gdn, s_p, cb_p = _gdn_prompt(gqkv.reshape(b, t, 3 * gw), z.reshape(b, t, gw), ab.reshape(b, t, LANES),
                                   conv_w[0], alog_pad, dtb_pad, gdn_norm_w[0], _row_tile(t, 512))
    h1, ox = _mid(o_moba.reshape(mp, mw), o_gdn.reshape(mp, gw), xp, w_out_b, norm_x_w[0], w_xq_b, _row_tile(t, 512),
                  memory=(mk_p.reshape(b, n_mem, xw), mv_p.reshape(b, n_mem, xw)))
    y_p = _tail(h1, ox, w_xo_b, norm_ffn_w[0], w_gu_b, w_down_b, final_norm_w, _row_tile(mp, 512))

    xs = x_sample.reshape(db, dm)
    qs, ks, vs, gqkv_s, z_s, ab_s = _norm_matmul(xs, norm_mix_w[0], w_in_hi, w_in_lo, in_splits, db)
    o_moba_s = _moba_sample(qs, ks, vs, cache_k[0], cache_v[0], page_table, slopes)
    o_gdn_s, s_s, cb_s = _gdn_sample(gqkv_s, z_s, ab_s, state_conv[0], state_gdn[0], conv_w[0], alog_pad, dtb_pad,
                                     gdn_norm_w[0])
    h1_s, qx_s = _mid(o_moba_s, o_gdn_s, xs, w_out_b, norm_x_w[0], w_xq_b, db)
    ox_s = _xattn_sample(qx_s, cache_mem_k[0], cache_mem_v[0])
    y_s = _tail(h1_s, ox_s, w_xo_b, norm_ffn_w[0], w_gu_b, w_down_b, final_norm_w, db)

    hh, hd = MOBA_HEADS, MOBA_HEAD_DIM
    xh = X_HEADS
    return (y_p.reshape(b, t, dm), y_s.reshape(db, 1, dm),
            jnp.transpose(kt, (0, 3, 1, 2))[None], jnp.transpose(vt, (0, 3, 1, 2))[None],
            cb_p[None], s_p[None],
            mk_p.reshape(1, b, n_mem, xh, xw // xh), mv_p.reshape(1, b, n_mem, xh, xw // xh),
            ks.reshape(1, db, 1, hh, hd), vs.reshape(1, db, 1, hh, hd),
            cb_s[None], s_s[None])
```
